```python
import jax, jax.numpy as jnp
from jax import lax
import numpy as np

D_MODEL = 1024
BATCH = 8
SEQ = 4096
DEPTH = 4

CTX_LEN = 256
GRID_W = 64
N_EVEN = (DEPTH + 1) // 2
N_ODD = DEPTH // 2
EPS = 1e-6

M_HEADS = 4
M_HEAD_DIM = D_MODEL // M_HEADS
M_WIDTH = M_HEADS * M_HEAD_DIM
M_CHUNK = 128
M_CONV = 3
P_WINDOWS = (2, 4, 8, 16)
P_GROUPS = len(P_WINDOWS)
P_GROUP_DIM = D_MODEL // P_GROUPS
P_WIDTH = P_GROUPS * P_GROUP_DIM
AB_WIDTH = M_WIDTH + P_WIDTH
N_GATE = 4 * M_HEADS
GATE_OFF = 5 * M_WIDTH + 2 * P_WIDTH
AB_IN = GATE_OFF + N_GATE

A_HEADS = 16
A_KV_HEADS = 4
A_GROUP = A_HEADS // A_KV_HEADS
A_HEAD_DIM = D_MODEL // A_HEADS
A_WIDTH = A_HEADS * A_HEAD_DIM
A_KV_WIDTH = A_KV_HEADS * A_HEAD_DIM
A_WINDOW = 128
A_BLOCK = 128
A_BAND = A_BLOCK + 2 * A_WINDOW
ROPE_BASE = 10000.0
C_IN = 2 * A_WIDTH + 2 * A_KV_WIDTH

kernel_name = 'hybrid_mlstm_pool_swa_diffusion_trunk'

f32 = jnp.float32


def rmsnorm(x, g):
    xf = x.astype(f32)
    return (xf * lax.rsqrt(jnp.mean(xf * xf, -1, keepdims=True) + EPS) * g).astype(x.dtype)


def short_conv(u, w):
    pad = w.shape[0] // 2
    return lax.conv_general_dilated(u, w[:, None, :].astype(u.dtype), (1,), [(pad, pad)],
                                    dimension_numbers=('NWC', 'WIO', 'NWC'),
                                    feature_group_count=u.shape[-1])


def to_heads(a):
    bsz, n, _ = a.shape
    return a.reshape(bsz, n, M_HEADS, M_HEAD_DIM).transpose(0, 2, 1, 3).astype(f32)


def mlstm_chunkwise(q, k, v, log_i, log_f, state):
    bsz, nh, n, dh = q.shape
    nc = n // M_CHUNK

    def chunks(a):
        return jnp.moveaxis(a.reshape(a.shape[:2] + (nc, M_CHUNK) + a.shape[3:]), 2, 0)

    lower = jnp.tril(jnp.ones((M_CHUNK, M_CHUNK), bool))

    def step(carry, inp):
        c_mem, n_mem, m_prev = carry
        qc, kc, vc, lic, lfc = inp
        b = jnp.cumsum(lfc, axis=-1)
        dmat = b[..., :, None] - b[..., None, :] + lic[..., None, :]
        dmat = jnp.where(lower, dmat, -jnp.inf)
        inter = b + m_prev[..., None]
        m_t = jnp.maximum(inter, dmat.max(-1))
        w_inter = jnp.exp(inter - m_t)
        s = jnp.einsum('bhtd,bhsd->bhts', qc, kc) * jnp.exp(dmat - m_t[..., None])
        num = w_inter[..., None] * jnp.einsum('bhtd,bhde->bhte', qc, c_mem) + jnp.einsum('bhts,bhse->bhte', s, vc)
        den = w_inter * jnp.einsum('bhtd,bhd->bht', qc, n_mem) + s.sum(-1)
        h = num / jnp.maximum(jnp.abs(den), jnp.exp(-m_t))[..., None]
        b_last = b[..., -1]
        d_last = b_last[..., None] - b + lic
        m_new = jnp.maximum(b_last + m_prev, d_last.max(-1))
        w_s = jnp.exp(d_last - m_new[..., None])[..., None] * kc
        decay = jnp.exp(b_last + m_prev - m_new)
        c_new = decay[..., None, None] * c_mem + jnp.einsum('bhsd,bhse->bhde', w_s, vc)
        n_new = decay[..., None] * n_mem + w_s.sum(2)
        return (c_new, n_new, m_new), h

    state, hs = lax.scan(step, state, (chunks(q), chunks(k), chunks(v), chunks(log_i), chunks(log_f)))
    return jnp.moveaxis(hs, 0, 2).reshape(bsz, nh, n, dh), state


def mlstm_inputs(u, conv_w, b_gate):
    bsz, n, _ = u.shape
    qk = short_conv(u[..., :2 * M_WIDTH], conv_w)
    q = to_heads(qk[..., :M_WIDTH])
    k = to_heads(qk[..., M_WIDTH:]) * (M_HEAD_DIM ** -0.5)
    v = to_heads(u[..., 2 * M_WIDTH:3 * M_WIDTH])
    g = (u[..., GATE_OFF:] + b_gate).astype(f32).reshape(bsz, n, 4, M_HEADS)
    g = jnp.transpose(g, (2, 0, 3, 1))
    return q, k, v, (g[0], jax.nn.log_sigmoid(g[1]), g[2], jax.nn.log_sigmoid(g[3]))


def multiscale_pool(u, pool_w, pool_scale):
    bsz, n, _ = u.shape
    ug = u.astype(f32).reshape(bsz, n, P_GROUPS, P_GROUP_DIM)
    csum = jnp.concatenate([jnp.zeros_like(ug[:, :1]), jnp.cumsum(ug, axis=1)], axis=1)
    t = jnp.arange(n)[:, None]
    half = jnp.array(P_WINDOWS, jnp.int32)[None, :] // 2
    lo = jnp.clip(t - half, 0, n)
    hi = jnp.clip(t + half, 0, n)
    gi = jnp.arange(P_GROUPS)[None, :]
    win_sum = csum[:, hi, gi] - csum[:, lo, gi]
    mean = win_sum / (hi - lo).astype(f32)[None, :, :, None]
    mixed = jnp.einsum('bngc,gcd->bngd', mean - ug, pool_w.astype(f32))
    return mixed.reshape(bsz, n, P_WIDTH) * pool_scale


def head_rmsnorm(h, w):
    bsz, n, _ = h.shape
    hh = h.reshape(bsz, n, M_HEADS, M_HEAD_DIM)
    hh = hh * lax.rsqrt(jnp.mean(hh * hh, -1, keepdims=True) + EPS)
    return hh.reshape(bsz, n, M_WIDTH) * w


def mlstm_pool_output(u, h, mnorm_w, pool_w, pool_scale, w_out):
    bsz, n, _ = u.shape
    o_gate = jax.nn.sigmoid(u[..., 3 * M_WIDTH:4 * M_WIDTH].astype(f32))
    h = jnp.transpose(h, (0, 2, 1, 3)).reshape(bsz, n, M_WIDTH) * o_gate
    y_m = head_rmsnorm(h, mnorm_w) * jax.nn.silu(u[..., 4 * M_WIDTH:5 * M_WIDTH].astype(f32))
    p0 = 5 * M_WIDTH
    y_p = multiscale_pool(u[..., p0:p0 + P_WIDTH], pool_w, pool_scale) * jax.nn.silu(
        u[..., p0 + P_WIDTH:p0 + 2 * P_WIDTH].astype(f32))
    return jnp.concatenate([y_m, y_p], -1).astype(u.dtype) @ w_out


def mlstm_pool_mixer(hx, hc, w_in, b_gate, conv_w, mnorm_w, pool_w, pool_scale, w_out, need_ctx):
    ux, uc = hx @ w_in, hc @ w_in
    qx, kx, vx, (lif_x, lff_x, lib_x, lfb_x) = mlstm_inputs(ux, conv_w, b_gate)
    qc, kc, vc, (lif_c, lff_c, lib_c, lfb_c) = mlstm_inputs(uc, conv_w, b_gate)
    bsz = hx.shape[0]
    zero = (jnp.zeros((bsz, M_HEADS, M_HEAD_DIM, M_HEAD_DIM), f32),
            jnp.zeros((bsz, M_HEADS, M_HEAD_DIM), f32),
            jnp.zeros((bsz, M_HEADS), f32))
    rev = lambda a: jnp.flip(a, axis=2)
    hc_f, st_f = mlstm_chunkwise(qc, kc, vc, lif_c, lff_c, zero)
    hx_f, _ = mlstm_chunkwise(qx, kx, vx, lif_x, lff_x, st_f)
    hc_b, st_b = mlstm_chunkwise(rev(qc), rev(kc), rev(vc), rev(lib_c), rev(lfb_c), zero)
    hx_b, _ = mlstm_chunkwise(rev(qx), rev(kx), rev(vx), rev(lib_x), rev(lfb_x), st_b)
    yx = mlstm_pool_output(ux, hx_f + rev(hx_b), mnorm_w, pool_w, pool_scale, w_out)
    yc = mlstm_pool_output(uc, hc_f + rev(hc_b), mnorm_w, pool_w, pool_scale, w_out) if need_ctx else None
    return yx, yc


def axial_rope(n):
    rows = n // GRID_W
    row = jnp.repeat(jnp.arange(rows), GRID_W).astype(f32)
    col = jnp.tile(jnp.arange(GRID_W), rows).astype(f32)
    n_freq = A_HEAD_DIM // 4
    inv = ROPE_BASE ** (-jnp.arange(n_freq, dtype=f32) / n_freq)
    ang = jnp.concatenate([row[:, None] * inv, col[:, None] * inv], -1)
    return jnp.cos(ang), jnp.sin(ang)


def apply_rope(x, cos, sin):
    xf = x.astype(f32)
    x1, x2 = xf[..., 0::2], xf[..., 1::2]
    c, s = cos[None, :, None, :], sin[None, :, None, :]
    return jnp.stack([x1 * c - x2 * s, x1 * s + x2 * c], -1).reshape(x.shape).astype(x.dtype)


def gqa_scores(q, k):
    return jnp.einsum('bqkgd,bskd->bkgqs', q, k).astype(f32) * (A_HEAD_DIM ** -0.5)


def window_attention(q, k, v, k_ctx, v_ctx, sink):
    bsz, n = q.shape[:2]
    n_ctx = k_ctx.shape[1]
    pad = ((0, 0), (A_WINDOW, A_WINDOW), (0, 0), (0, 0))
    kp, vp = jnp.pad(k, pad), jnp.pad(v, pad)
    qg = q.reshape(bsz, n, A_KV_HEADS, A_GROUP, A_HEAD_DIM)
    sink_l = sink.reshape(1, A_KV_HEADS, A_GROUP, 1, 1).astype(f32)

    def block(bi):
        start = bi * A_BLOCK
        qb = lax.dynamic_slice_in_dim(qg, start, A_BLOCK, axis=1)
        kb = lax.dynamic_slice_in_dim(kp, start, A_BAND, axis=1)
        vb = lax.dynamic_slice_in_dim(vp, start, A_BAND, axis=1)
        qi = start + jnp.arange(A_BLOCK)
        kj = start - A_WINDOW + jnp.arange(A_BAND)
        ok = (jnp.abs(qi[:, None] - kj[None, :]) <= A_WINDOW) & (kj >= 0) & (kj < n)
        s_lat = jnp.where(ok, gqa_scores(qb, kb), -jnp.inf)
        s_ctx = gqa_scores(qb, k_ctx)
        sk = jnp.broadcast_to(sink_l, s_ctx.shape[:-1] + (1,))
        p = jax.nn.softmax(jnp.concatenate([s_lat, s_ctx, sk], -1), axis=-1)
        o = (jnp.einsum('bkgqs,bskd->bqkgd', p[..., :A_BAND].astype(v.dtype), vb)
             + jnp.einsum('bkgqs,bskd->bqkgd', p[..., A_BAND:A_BAND + n_ctx].astype(v.dtype), v_ctx))
        return o.reshape(bsz, A_BLOCK, A_WIDTH)

    out = lax.map(block, jnp.arange(n // A_BLOCK))
    return jnp.moveaxis(out, 0, 1).reshape(bsz, n, A_WIDTH)


def context_attention(q, k, v, sink):
    bsz, n = q.shape[:2]
    s = gqa_scores(q.reshape(bsz, n, A_KV_HEADS, A_GROUP, A_HEAD_DIM), k)
    sk = jnp.broadcast_to(sink.reshape(1, A_KV_HEADS, A_GROUP, 1, 1).astype(f32), s.shape[:-1] + (1,))
    p = jax.nn.softmax(jnp.concatenate([s, sk], -1), axis=-1)
    o = jnp.einsum('bkgqs,bskd->bqkgd', p[..., :n].astype(v.dtype), v)
    return o.reshape(bsz, n, A_WIDTH)


def split_attn(u):
    bsz, n, _ = u.shape
    q = u[..., :A_WIDTH].reshape(bsz, n, A_HEADS, A_HEAD_DIM)
    k = u[..., A_WIDTH:A_WIDTH + A_KV_WIDTH].reshape(bsz, n, A_KV_HEADS, A_HEAD_DIM)
    v = u[..., A_WIDTH + A_KV_WIDTH:A_WIDTH + 2 * A_KV_WIDTH].reshape(bsz, n, A_KV_HEADS, A_HEAD_DIM)
    z = u[..., A_WIDTH + 2 * A_KV_WIDTH:]
    return q, k, v, z


def window_attention_mixer(hx, hc, w_in, sink, w_out, need_ctx):
    qx, kx, vx, zx = split_attn(hx @ w_in)
    qc, kc, vc, zc = split_attn(hc @ w_in)
    cos, sin = axial_rope(hx.shape[1])
    qx, kx = apply_rope(qx, cos, sin), apply_rope(kx, cos, sin)
    ox = window_attention(qx, kx, vx, kc, vc, sink)
    yx = (ox * jax.nn.silu(zx)) @ w_out
    yc = (context_attention(qc, kc, vc, sink) * jax.nn.silu(zc)) @ w_out if need_ctx else None
    return yx, yc


def setup_inputs(seed: int = 0) -> dict:
    key = jax.random.key(seed)
    ks = jax.random.split(key, 20)
    nrm = lambda k, shape, s: jax.random.normal(k, shape, f32) * s
    kg = jax.random.split(ks[9], 2)
    i_bias = nrm(kg[0], (N_EVEN, 2, M_HEADS), 0.1)
    f_bias = jnp.linspace(3.0, 6.0, M_HEADS, dtype=f32) + nrm(kg[1], (N_EVEN, 2, M_HEADS), 0.1)
    return {
        'x': nrm(ks[0], (BATCH, SEQ, D_MODEL), 1.0),
        'c': nrm(ks[1], (BATCH, D_MODEL), 1.0),
        'ctx': nrm(ks[2], (BATCH, CTX_LEN, D_MODEL), 1.0),
        'c_ctx': nrm(ks[3], (D_MODEL,), 1.0),
        'w_mod': nrm(ks[4], (DEPTH, D_MODEL, 3 * D_MODEL), 0.5 * D_MODEL ** -0.5),
        'b_mod': nrm(ks[5], (DEPTH, 3 * D_MODEL), 0.01),
        'g_pre': 1.0 + nrm(ks[6], (DEPTH, D_MODEL), 0.02),
        'g_post': 1.0 + nrm(ks[7], (DEPTH, D_MODEL), 0.02),
        'ab_w_in': nrm(ks[8], (N_EVEN, D_MODEL, AB_IN), D_MODEL ** -0.5),
        'ab_b_gate': jnp.stack([i_bias, f_bias], axis=2).reshape(N_EVEN, N_GATE),
        'ab_conv': nrm(ks[10], (N_EVEN, M_CONV, 2 * M_WIDTH), M_CONV ** -0.5),
        'ab_mnorm': 1.0 + nrm(ks[11], (N_EVEN, M_WIDTH), 0.02),
        'ab_pool_w': nrm(ks[12], (N_EVEN, P_GROUPS, P_GROUP_DIM, P_GROUP_DIM), P_GROUP_DIM ** -0.5),
        'ab_pool_scale': 1.0 + nrm(ks[13], (N_EVEN, P_WIDTH), 0.02),
        'ab_w_out': nrm(ks[14], (N_EVEN, AB_WIDTH, D_MODEL), AB_WIDTH ** -0.5),
        'c_w_in': nrm(ks[15], (N_ODD, D_MODEL, C_IN), D_MODEL ** -0.5),
        'c_sink': nrm(ks[16], (N_ODD, A_HEADS), 1.0),
        'c_w_out': nrm(ks[17], (N_ODD, A_WIDTH, D_MODEL), A_WIDTH ** -0.5),
    }


def reference(x, c, ctx, c_ctx, w_mod, b_mod, g_pre, g_post, ab_w_in, ab_b_gate, ab_conv, ab_mnorm,
              ab_pool_w, ab_pool_scale, ab_w_out, c_w_in, c_sink, c_w_out):
    for l in range(DEPTH):
        j = l // 2
        last = l == DEPTH - 1
        shift, scale, gate = jnp.split(jax.nn.silu(c) @ w_mod[l] + b_mod[l], 3, axis=-1)
        shift_c, scale_c, gate_c = jnp.split(jax.nn.silu(c_ctx) @ w_mod[l] + b_mod[l], 3, axis=-1)
        hx = rmsnorm(x, g_pre[l]) * (1 + scale[:, None]) + shift[:, None]
        hc = rmsnorm(ctx, g_pre[l]) * (1 + scale_c) + shift_c
        if l % 2 == 0:
            yx, yc = mlstm_pool_mixer(hx, hc, ab_w_in[j], ab_b_gate[j], ab_conv[j], ab_mnorm[j],
                                      ab_pool_w[j], ab_pool_scale[j], ab_w_out[j], not last)
        else:
            yx, yc = window_attention_mixer(hx, hc, c_w_in[j], c_sink[j], c_w_out[j], not last)
        x = x + gate[:, None] * rmsnorm(yx, g_post[l])
        if not last:
            ctx = ctx + gate_c * rmsnorm(yc, g_post[l])
    return x
```

```python
import functools

import jax
import jax.numpy as jnp
import numpy as np
from jax import lax
from jax.experimental import pallas as pl
from jax.experimental.pallas import tpu as pltpu

f32 = jnp.float32
bf16 = jnp.bfloat16

D_MODEL = 1024
DEPTH = 4
EPS = 1e-6
GRID_W = 64

M_HEADS = 4
M_HEAD_DIM = 256
M_WIDTH = 1024
M_CHUNK = 128
P_WINDOWS = (2, 4, 8, 16)
P_GROUP_DIM = 256
P_WIDTH = 1024
N_GATE = 16
GATE_OFF = 5 * M_WIDTH + 2 * P_WIDTH
GATE_PAD = 128

A_HEADS = 16
A_KV_HEADS = 4
A_GROUP = 4
A_HEAD_DIM = 64
A_WIDTH = 1024
A_KV_WIDTH = 256
A_WINDOW = 128
A_BLOCK = 128
A_BAND = A_BLOCK + 2 * A_WINDOW
ROPE_BASE = 10000.0

BF16_ROWS = 16
VMEM_LIMIT = 52 * 1024 * 1024


def _params(*sem):
    return pltpu.CompilerParams(dimension_semantics=sem, vmem_limit_bytes=VMEM_LIMIT)


def _silu(a):
    return a * jax.nn.sigmoid(a)


def _modulated_norm(x, gpre, mod):
    xn = x * lax.rsqrt(jnp.mean(x * x, -1, keepdims=True) + EPS) * gpre
    return xn * (1.0 + mod[:, D_MODEL:2 * D_MODEL]) + mod[:, :D_MODEL]


def _gated_residual(x, y, gpost, mod):
    r = y * lax.rsqrt(jnp.mean(y * y, -1, keepdims=True) + EPS) * gpost
    return x + mod[:, 2 * D_MODEL:] * r


def _mod_kernel(cs_ref, w_ref, b_ref, o_ref):
    a = _silu(cs_ref[...])
    o_ref[0] = jnp.dot(a, w_ref[0], preferred_element_type=f32,
                       precision=lax.Precision.HIGHEST) + b_ref[0]


def _modulation(cs, w_mod, b_mod):
    tn = 512
    return pl.pallas_call(
        _mod_kernel,
        grid=(DEPTH, 3 * D_MODEL // tn),
        in_specs=[pl.BlockSpec((16, D_MODEL), lambda l, j: (0, 0)),
                  pl.BlockSpec((1, D_MODEL, tn), lambda l, j: (l, 0, j)),
                  pl.BlockSpec((1, 1, tn), lambda l, j: (l, 0, j))],
        out_specs=pl.BlockSpec((1, 16, tn), lambda l, j: (l, 0, j)),
        out_shape=jax.ShapeDtypeStruct((DEPTH, 16, 3 * D_MODEL), f32),
        compiler_params=_params("parallel", "parallel"),
        name="modulation",
    )(cs, w_mod, b_mod.reshape(DEPTH, 1, 3 * D_MODEL))


def _in_ab_kernel(x_ref, mod_ref, gpre_ref, w_ref, wg_ref, bg_ref, u_ref, g_ref, h_scr):
    @pl.when(pl.program_id(2) == 0)
    def _():
        hb = _modulated_norm(x_ref[0], gpre_ref[...], mod_ref[0]).astype(bf16)
        h_scr[...] = hb
        g_ref[0] = jnp.dot(hb, wg_ref[...], preferred_element_type=f32) + bg_ref[...]

    u_ref[0] = jnp.dot(h_scr[...], w_ref[...], preferred_element_type=f32).astype(bf16)


def _in_ab(x, mod, gpre, w, wg, bg, tm):
    bsz, n, _ = x.shape
    tn = 1024
    return pl.pallas_call(
        _in_ab_kernel,
        grid=(bsz, n // tm, GATE_OFF // tn),
        in_specs=[pl.BlockSpec((1, tm, D_MODEL), lambda b, i, j: (b, i, 0)),
                  pl.BlockSpec((1, 1, 3 * D_MODEL), lambda b, i, j: (b % mod.shape[0], 0, 0)),
                  pl.BlockSpec((1, D_MODEL), lambda b, i, j: (0, 0)),
                  pl.BlockSpec((D_MODEL, tn), lambda b, i, j: (0, j)),
                  pl.BlockSpec((D_MODEL, GATE_PAD), lambda b, i, j: (0, 0)),
                  pl.BlockSpec((1, GATE_PAD), lambda b, i, j: (0, 0))],
        out_specs=[pl.BlockSpec((1, tm, tn), lambda b, i, j: (b, i, j)),
                   pl.BlockSpec((1, tm, GATE_PAD), lambda b, i, j: (b, i, 0))],
        out_shape=[jax.ShapeDtypeStruct((bsz, n, GATE_OFF), bf16),
                   jax.ShapeDtypeStruct((bsz, n, GATE_PAD), f32)],
        scratch_shapes=[pltpu.VMEM((tm, D_MODEL), bf16)],
        compiler_params=_params("parallel", "parallel", "arbitrary"),
        name="in_proj_ab",
    )(x, mod, gpre, w, wg, bg)


def _log_sigmoid(a):
    return jnp.minimum(a, 0.0) - jnp.log1p(jnp.exp(-jnp.abs(a)))


def _gate_rows(g_ref, r_scr, head):
    g = g_ref[0]
    n = g.shape[0]
    sub = lax.broadcasted_iota(jnp.int32, (8, GATE_PAD), 0)
    lane = lax.broadcasted_iota(jnp.int32, (8, GATE_PAD), 1)
    pick = jnp.where((lane == sub * M_HEADS + head) & (sub < 4), 1.0, 0.0).astype(bf16)
    rows = jnp.zeros((8, n), f32)
    rest = g
    for _ in range(3):
        piece = rest.astype(bf16)
        rest = rest - piece.astype(f32)
        rows = rows + lax.dot_general(pick, piece, (((1,), (1,)), ((), ())),
                                      preferred_element_type=f32)
    logf = _log_sigmoid(rows)
    pos = lax.broadcasted_iota(jnp.int32, (8, n), 1) % M_CHUNK
    pre, suf = logf, logf
    shift = 1
    while shift < M_CHUNK:
        pre = pre + jnp.where(pos >= shift, pltpu.roll(pre, shift, 1), 0.0)
        suf = suf + jnp.where(pos < M_CHUNK - shift, pltpu.roll(suf, n - shift, 1), 0.0)
        shift *= 2
    row = lax.broadcasted_iota(jnp.int32, (8, n), 0)
    r_scr[...] = jnp.where(row == 1, pre, jnp.where(row == 3, suf, rows))


def _row_to_col(row, eye):
    return jnp.sum(jnp.where(eye, row, 0.0), axis=-1, keepdims=True)


def _conv3(u_ref, w_ref, start, n_seq):
    static = isinstance(start, int)
    u = u_ref[0, pl.ds(start, M_CHUNK), :].astype(f32)
    if static:
        ps, ns = max(start - BF16_ROWS, 0), min(start + M_CHUNK, n_seq - BF16_ROWS)
    else:
        ps = pl.multiple_of(jnp.maximum(start - BF16_ROWS, 0), BF16_ROWS)
        ns = pl.multiple_of(jnp.minimum(start + M_CHUNK, n_seq - BF16_ROWS), BF16_ROWS)
    prev = u_ref[0, pl.ds(ps, BF16_ROWS), :].astype(f32)[BF16_ROWS - 1:BF16_ROWS]
    nxt = u_ref[0, pl.ds(ns, BF16_ROWS), :].astype(f32)[0:1]
    prev = jnp.where(start > 0, prev, 0.0)
    nxt = jnp.where(start + M_CHUNK < n_seq, nxt, 0.0)
    rid = lax.broadcasted_iota(jnp.int32, (M_CHUNK, 1), 0)
    up = jnp.where(rid == 0, prev, pltpu.roll(u, 1, 0))
    un = jnp.where(rid == M_CHUNK - 1, nxt, pltpu.roll(u, M_CHUNK - 1, 0))
    return w_ref[0:1, :] * up + w_ref[1:2, :] * u + w_ref[2:3, :] * un


def _mlstm_chunk(seg, consts, state, start, direction, first):
    q_ref, k_ref, v_ref, o_ref, z_ref, r_scr, acc, y_ref, n_seq = seg
    cq_ref, ck_ref, mn_ref = consts
    c_scr, n_scr, m_scr = state
    if not isinstance(start, int):
        start = pl.multiple_of(start, M_CHUNK)
    rows = pl.ds(start, M_CHUNK)

    q = _conv3(q_ref, cq_ref, start, n_seq)
    k = _conv3(k_ref, ck_ref, start, n_seq) * (M_HEAD_DIM ** -0.5)
    kt = k.T
    qb = q.astype(bf16)
    v = v_ref[0, rows, :]

    li = r_scr[2 * direction:2 * direction + 1, rows]
    bcum = r_scr[2 * direction + 1:2 * direction + 2, rows]
    ri = lax.broadcasted_iota(jnp.int32, (M_CHUNK, M_CHUNK), 0)
    ci = lax.broadcasted_iota(jnp.int32, (M_CHUNK, M_CHUNK), 1)
    eye = ri == ci
    causal = (ci <= ri) if direction == 0 else (ci >= ri)
    b_last = bcum[:, M_CHUNK - 1:M_CHUNK] if direction == 0 else bcum[:, 0:1]

    m_prev = m_scr[direction]
    c_mem = c_scr[direction]
    n_mem = n_scr[direction]

    b_col = _row_to_col(bcum, eye)
    dmat = jnp.where(causal, b_col + (li - bcum), -jnp.inf)
    inter = b_col + m_prev
    m_t = jnp.maximum(inter, jnp.max(dmat, axis=-1, keepdims=True))
    w_inter = jnp.exp(inter - m_t)
    s = jnp.dot(qb, kt.astype(bf16), preferred_element_type=f32) * jnp.exp(dmat - m_t)
    num = (w_inter * jnp.dot(qb, c_mem.astype(bf16), preferred_element_type=f32)
           + jnp.dot(s.astype(bf16), v, preferred_element_type=f32))
    den = (w_inter * jnp.sum(q * n_mem, axis=-1, keepdims=True)
           + jnp.sum(s, axis=-1, keepdims=True))
    h = num / jnp.maximum(jnp.abs(den), jnp.exp(-m_t))

    d_last = b_last - bcum + li
    m_new = jnp.maximum(b_last + m_prev, jnp.max(d_last, axis=-1, keepdims=True))
    w_row = jnp.exp(d_last - m_new)
    decay = jnp.exp(b_last + m_prev - m_new)
    c_scr[direction] = decay * c_mem + jnp.dot((kt * w_row).astype(bf16), v,
                                               preferred_element_type=f32)
    n_scr[direction] = decay * n_mem + jnp.sum(k * _row_to_col(w_row, eye), axis=0, keepdims=True)
    m_scr[direction] = m_new

    if first:
        acc[rows, :] = h
    else:
        hg = (acc[rows, :] + h) * jax.nn.sigmoid(o_ref[0, rows, :].astype(f32))
        hn = hg * lax.rsqrt(jnp.mean(hg * hg, -1, keepdims=True) + EPS) * mn_ref[...]
        y_ref[0, rows, :] = (hn * _silu(z_ref[0, rows, :].astype(f32))).astype(bf16)


def _mlstm_kernel(qx_ref, kx_ref, vx_ref, ox_ref, zx_ref, gx_ref,
                  qc_ref, kc_ref, vc_ref, oc_ref, zc_ref, gc_ref,
                  cq_ref, ck_ref, mn_ref, yx_ref, yc_ref,
                  rx_scr, rc_scr, accx, accc, c_scr, n_scr, m_scr):
    head = pl.program_id(1)
    nx, nc = qx_ref.shape[1], qc_ref.shape[1]
    _gate_rows(gx_ref, rx_scr, head)
    _gate_rows(gc_ref, rc_scr, head)
    c_scr[...] = jnp.zeros_like(c_scr)
    n_scr[...] = jnp.zeros_like(n_scr)
    m_scr[...] = jnp.zeros_like(m_scr)

    consts = (cq_ref, ck_ref, mn_ref)
    state = (c_scr, n_scr, m_scr)
    ctx = (qc_ref, kc_ref, vc_ref, oc_ref, zc_ref, rc_scr, accc, yc_ref, nc)
    lat = (qx_ref, kx_ref, vx_ref, ox_ref, zx_ref, rx_scr, accx, yx_ref, nx)
    step = functools.partial(_mlstm_chunk, consts=consts, state=state)

    ncc = nc // M_CHUNK
    assert ncc == 2
    step(ctx, start=0, direction=0, first=True)
    step(ctx, start=M_CHUNK, direction=1, first=True)
    step(ctx, start=M_CHUNK, direction=0, first=False)
    step(ctx, start=0, direction=1, first=False)

    nxc = nx // M_CHUNK

    def meet(first):
        def body(i, carry):
            step(lat, start=i * M_CHUNK, direction=0, first=first)
            step(lat, start=(nxc - 1 - i) * M_CHUNK, direction=1, first=first)
            return carry
        return body

    lax.fori_loop(0, nxc // 2, meet(True), 0)
    lax.fori_loop(nxc // 2, nxc, meet(False), 0)


def _mlstm(ux, gx, uc, gc, conv_w, mnorm):
    bsz, nx, _ = ux.shape
    nc = uc.shape[1]
    hd = M_HEAD_DIM

    def sect(n, k):
        return pl.BlockSpec((1, n, hd), lambda b, h, k=k: (b, 0, k * M_HEADS + h))

    def gates(n):
        return pl.BlockSpec((1, n, GATE_PAD), lambda b, h: (b, 0, 0))

    in_specs = ([sect(nx, k) for k in range(5)] + [gates(nx)]
                + [sect(nc, k) for k in range(5)] + [gates(nc)]
                + [pl.BlockSpec((3, hd), lambda b, h: (0, h)),
                   pl.BlockSpec((3, hd), lambda b, h: (0, M_HEADS + h)),
                   pl.BlockSpec((1, hd), lambda b, h: (0, h))])
    out_specs = [pl.BlockSpec((1, nx, hd), lambda b, h: (b, 0, h)),
                 pl.BlockSpec((1, nc, hd), lambda b, h: (b, 0, h))]
    return pl.pallas_call(
        _mlstm_kernel,
        grid=(bsz, M_HEADS),
        in_specs=in_specs,
        out_specs=out_specs,
        out_shape=[jax.ShapeDtypeStruct((bsz, nx, M_WIDTH), bf16),
                   jax.ShapeDtypeStruct((bsz, nc, M_WIDTH), bf16)],
        scratch_shapes=[pltpu.VMEM((8, nx), f32), pltpu.VMEM((8, nc), f32),
                        pltpu.VMEM((nx, hd), f32), pltpu.VMEM((nc, hd), f32),
                        pltpu.VMEM((2, hd, hd), f32), pltpu.VMEM((2, 1, hd), f32),
                        pltpu.VMEM((2, 1, 1), f32)],
        compiler_params=_params("parallel", "parallel"),
        name="mlstm",
    )(ux, ux, ux, ux, ux, gx, uc, uc, uc, uc, uc, gc, conv_w, conv_w, mnorm)


P_HALO = 16


def _out_ab_kernel(ym_ref, xp_ref, xpp_ref, xpn_ref, zp_ref, x_ref, mod_ref, gpost_ref,
                   wo_ref, pw_ref, ps_ref, o_ref, ext_scr):
    i, nt = pl.program_id(1), pl.num_programs(1)
    tm = ym_ref.shape[1]
    ext_scr[0:P_HALO, :] = jnp.where(i > 0, xpp_ref[0].astype(f32), 0.0)
    ext_scr[P_HALO:P_HALO + tm, :] = xp_ref[0].astype(f32)
    ext_scr[P_HALO + tm:, :] = jnp.where(i < nt - 1, xpn_ref[0].astype(f32), 0.0)

    t = i * tm + lax.broadcasted_iota(jnp.int32, (tm, 1), 0)
    n = nt * tm
    y = jnp.dot(ym_ref[0], wo_ref[0:M_WIDTH, :], preferred_element_type=f32)
    for g, win in enumerate(P_WINDOWS):
        half = win // 2
        cols = slice(g * P_GROUP_DIM, (g + 1) * P_GROUP_DIM)
        total = ext_scr[P_HALO - half:P_HALO - half + tm, cols]
        for d in range(-half + 1, half):
            total = total + ext_scr[P_HALO + d:P_HALO + d + tm, cols]
        count = (jnp.minimum(t + half, n) - jnp.maximum(t - half, 0)).astype(f32)
        pooled = total / count - ext_scr[P_HALO:P_HALO + tm, cols]
        mixed = jnp.dot(pooled.astype(bf16), pw_ref[g], preferred_element_type=f32) * ps_ref[:, cols]
        yp = (mixed * _silu(zp_ref[0, :, cols].astype(f32))).astype(bf16)
        y = y + jnp.dot(yp, wo_ref[M_WIDTH + g * P_GROUP_DIM:M_WIDTH + (g + 1) * P_GROUP_DIM, :],
                        preferred_element_type=f32)
    o_ref[0] = _gated_residual(x_ref[0], y, gpost_ref[...], mod_ref[0])


def _out_ab(ym, u, x, mod, gpost, wo, pw, ps, tm):
    bsz, n, _ = x.shape
    nh = n // P_HALO
    th = tm // P_HALO
    xp_col, zp_col = 5, 6
    return pl.pallas_call(
        _out_ab_kernel,
        grid=(bsz, n // tm),
        in_specs=[pl.BlockSpec((1, tm, M_WIDTH), lambda b, i: (b, i, 0)),
                  pl.BlockSpec((1, tm, P_WIDTH), lambda b, i: (b, i, xp_col)),
                  pl.BlockSpec((1, P_HALO, P_WIDTH), lambda b, i: (b, jnp.maximum(i * th - 1, 0), xp_col)),
                  pl.BlockSpec((1, P_HALO, P_WIDTH), lambda b, i: (b, jnp.minimum((i + 1) * th, nh - 1), xp_col)),
                  pl.BlockSpec((1, tm, P_WIDTH), lambda b, i: (b, i, zp_col)),
                  pl.BlockSpec((1, tm, D_MODEL), lambda b, i: (b, i, 0)),
                  pl.BlockSpec((1, 1, 3 * D_MODEL), lambda b, i: (b % mod.shape[0], 0, 0)),
                  pl.BlockSpec((1, D_MODEL), lambda b, i: (0, 0)),
                  pl.BlockSpec((M_WIDTH + P_WIDTH, D_MODEL), lambda b, i: (0, 0)),
                  pl.BlockSpec((len(P_WINDOWS), P_GROUP_DIM, P_GROUP_DIM), lambda b, i: (0, 0, 0)),
                  pl.BlockSpec((1, P_WIDTH), lambda b, i: (0, 0))],
        out_specs=pl.BlockSpec((1, tm, D_MODEL), lambda b, i: (b, i, 0)),
        out_shape=jax.ShapeDtypeStruct((bsz, n, D_MODEL), f32),
        scratch_shapes=[pltpu.VMEM((tm + 2 * P_HALO, P_WIDTH), f32)],
        compiler_params=_params("parallel", "parallel"),
        name="out_proj_ab",
    )(ym, u, u, u, u, x, mod, gpost, wo, pw, ps)


def _rope(a, cos, sin):
    lane = lax.broadcasted_iota(jnp.int32, a.shape, 1)
    half = A_HEAD_DIM // 2
    partner = jnp.where(lane % A_HEAD_DIM < half,
                        pltpu.roll(a, a.shape[1] - half, 1), pltpu.roll(a, half, 1))
    return a * cos + partner * sin


def _in_c_kernel(x_ref, mod_ref, gpre_ref, w_ref, cos_ref, sin_ref, q_ref, k_ref, v_ref, z_ref, *, rope):
    hb = _modulated_norm(x_ref[0], gpre_ref[...], mod_ref[0]).astype(bf16)

    def proj(lo, width):
        return jnp.dot(hb, w_ref[:, lo:lo + width], preferred_element_type=f32)

    def rotated(lo):
        a = proj(lo, 128)
        return _rope(a, cos_ref[...], sin_ref[...]) if rope else a

    for j in range(A_WIDTH // 128):
        q_ref[0, :, j * 128:(j + 1) * 128] = (rotated(j * 128) * (A_HEAD_DIM ** -0.5)).astype(bf16)
    for j in range(A_KV_WIDTH // 128):
        k_ref[0, :, j * 128:(j + 1) * 128] = rotated(A_WIDTH + j * 128).astype(bf16)
    v_ref[0] = proj(A_WIDTH + A_KV_WIDTH, A_KV_WIDTH).astype(bf16)
    z_ref[0] = proj(A_WIDTH + 2 * A_KV_WIDTH, A_WIDTH).astype(bf16)


def _in_c(x, mod, gpre, w, cos, sin, tm, rope):
    bsz, n, _ = x.shape
    c_in = w.shape[1]
    return pl.pallas_call(
        functools.partial(_in_c_kernel, rope=rope),
        grid=(bsz, n // tm),
        in_specs=[pl.BlockSpec((1, tm, D_MODEL), lambda b, i: (b, i, 0)),
                  pl.BlockSpec((1, 1, 3 * D_MODEL), lambda b, i: (b % mod.shape[0], 0, 0)),
                  pl.BlockSpec((1, D_MODEL), lambda b, i: (0, 0)),
                  pl.BlockSpec((D_MODEL, c_in), lambda b, i: (0, 0)),
                  pl.BlockSpec((tm, 128), lambda b, i: (i, 0)),
                  pl.BlockSpec((tm, 128), lambda b, i: (i, 0))],
        out_specs=[pl.BlockSpec((1, tm, A_WIDTH), lambda b, i: (b, i, 0)),
                   pl.BlockSpec((1, tm, A_KV_WIDTH), lambda b, i: (b, i, 0)),
                   pl.BlockSpec((1, tm, A_KV_WIDTH), lambda b, i: (b, i, 0)),
                   pl.BlockSpec((1, tm, A_WIDTH), lambda b, i: (b, i, 0))],
        out_shape=[jax.ShapeDtypeStruct((bsz, n, A_WIDTH), bf16),
                   jax.ShapeDtypeStruct((bsz, n, A_KV_WIDTH), bf16),
                   jax.ShapeDtypeStruct((bsz, n, A_KV_WIDTH), bf16),
                   jax.ShapeDtypeStruct((bsz, n, A_WIDTH), bf16)],
        compiler_params=_params("parallel", "parallel"),
        name="in_proj_c",
    )(x, mod, gpre, w, cos, sin)


def _attn_kernel(sink_ref, q_ref, k_ref, v_ref, kc_ref, vc_ref, z_ref, og_ref, o_scr, *, banded):
    nt = (((1,), (1,)), ((), ()))
    stacked = A_GROUP * A_BLOCK
    if banded:
        n = k_ref.shape[1]
        start = pl.program_id(1) * A_BLOCK
        bs = pl.multiple_of(jnp.clip(start - A_WINDOW, 0, n - A_BAND), A_BLOCK)
        qi = start + lax.broadcasted_iota(jnp.int32, (stacked, A_BAND), 0) % A_BLOCK
        kj = bs + lax.broadcasted_iota(jnp.int32, (stacked, A_BAND), 1)
        ok = jnp.abs(qi - kj) <= A_WINDOW
    for g in range(A_KV_HEADS):
        kv = slice(g * A_HEAD_DIM, (g + 1) * A_HEAD_DIM)
        heads = range(g * A_GROUP, (g + 1) * A_GROUP)
        qg = jnp.concatenate([q_ref[0, :, h * A_HEAD_DIM:(h + 1) * A_HEAD_DIM] for h in heads], axis=0)
        sk = jnp.concatenate([jnp.full((A_BLOCK, 1), sink_ref[h], f32) for h in heads], axis=0)
        s_ctx = lax.dot_general(qg, kc_ref[0, :, kv], nt, preferred_element_type=f32)
        m = jnp.maximum(jnp.max(s_ctx, axis=-1, keepdims=True), sk)
        if banded:
            s_lat = lax.dot_general(qg, k_ref[0, pl.ds(bs, A_BAND), kv], nt, preferred_element_type=f32)
            s_lat = jnp.where(ok, s_lat, -jnp.inf)
            m = jnp.maximum(m, jnp.max(s_lat, axis=-1, keepdims=True))
        e_ctx = jnp.exp(s_ctx - m)
        total = jnp.sum(e_ctx, axis=-1, keepdims=True) + jnp.exp(sk - m)
        o = jnp.dot(e_ctx.astype(bf16), vc_ref[0, :, kv], preferred_element_type=f32)
        if banded:
            e_lat = jnp.exp(s_lat - m)
            total = total + jnp.sum(e_lat, axis=-1, keepdims=True)
            o = o + jnp.dot(e_lat.astype(bf16), v_ref[0, pl.ds(bs, A_BAND), kv], preferred_element_type=f32)
        o = o / total
        for a, h in enumerate(heads):
            o_scr[:, h * A_HEAD_DIM:(h + 1) * A_HEAD_DIM] = o[a * A_BLOCK:(a + 1) * A_BLOCK]
    og_ref[0] = (o_scr[...] * _silu(z_ref[0].astype(f32))).astype(bf16)


def _attn(sink, q, k, v, kc, vc, z, banded):
    bsz, nq, _ = q.shape
    nk, nc = k.shape[1], kc.shape[1]
    return pl.pallas_call(
        functools.partial(_attn_kernel, banded=banded),
        grid=(bsz, nq // A_BLOCK),
        in_specs=[pl.BlockSpec(memory_space=pltpu.SMEM),
                  pl.BlockSpec((1, A_BLOCK, A_WIDTH), lambda b, i: (b, i, 0)),
                  pl.BlockSpec((1, nk, A_KV_WIDTH), lambda b, i: (b, 0, 0)),
                  pl.BlockSpec((1, nk, A_KV_WIDTH), lambda b, i: (b, 0, 0)),
                  pl.BlockSpec((1, nc, A_KV_WIDTH), lambda b, i: (b, 0, 0)),
                  pl.BlockSpec((1, nc, A_KV_WIDTH), lambda b, i: (b, 0, 0)),
                  pl.BlockSpec((1, A_BLOCK, A_WIDTH), lambda b, i: (b, i, 0))],
        out_specs=pl.BlockSpec((1, A_BLOCK, A_WIDTH), lambda b, i: (b, i, 0)),
        out_shape=jax.ShapeDtypeStruct((bsz, nq, A_WIDTH), bf16),
        scratch_shapes=[pltpu.VMEM((A_BLOCK, A_WIDTH), f32)],
        compiler_params=_params("parallel", "parallel"),
        name="attention_banded" if banded else "attention_context",
    )(sink, q, k, v, kc, vc, z)


def _out_c_kernel(a_ref, x_ref, mod_ref, gpost_ref, wo_ref, o_ref):
    y = jnp.dot(a_ref[0], wo_ref[...], preferred_element_type=f32)
    o_ref[0] = _gated_residual(x_ref[0], y, gpost_ref[...], mod_ref[0])


def _out_c(a, x, mod, gpost, wo, tm):
    bsz, n, _ = x.shape
    return pl.pallas_call(
        _out_c_kernel,
        grid=(bsz, n // tm),
        in_specs=[pl.BlockSpec((1, tm, A_WIDTH), lambda b, i: (b, i, 0)),
                  pl.BlockSpec((1, tm, D_MODEL), lambda b, i: (b, i, 0)),
                  pl.BlockSpec((1, 1, 3 * D_MODEL), lambda b, i: (b % mod.shape[0], 0, 0)),
                  pl.BlockSpec((1, D_MODEL), lambda b, i: (0, 0)),
                  pl.BlockSpec((A_WIDTH, D_MODEL), lambda b, i: (0, 0))],
        out_specs=pl.BlockSpec((1, tm, D_MODEL), lambda b, i: (b, i, 0)),
        out_shape=jax.ShapeDtypeStruct((bsz, n, D_MODEL), f32),
        compiler_params=_params("parallel", "parallel"),
        name="out_proj_c",
    )(a, x, mod, gpost, wo)


def _rope_tables(n):
    rows = n // GRID_W
    row = jnp.repeat(jnp.arange(rows), GRID_W).astype(f32)
    col = jnp.tile(jnp.arange(GRID_W), rows).astype(f32)
    n_freq = A_HEAD_DIM // 4
    inv = ROPE_BASE ** (-jnp.arange(n_freq, dtype=f32) / n_freq)
    ang = jnp.concatenate([row[:, None] * inv, col[:, None] * inv], -1)
    cos, sin = jnp.cos(ang), jnp.sin(ang)
    cos_t = jnp.tile(jnp.concatenate([cos, cos], -1), (1, 128 // A_HEAD_DIM))
    sin_t = jnp.tile(jnp.concatenate([-sin, sin], -1), (1, 128 // A_HEAD_DIM))
    return cos_t, sin_t


def _split_pairs_columns(w, heads):
    perm = np.concatenate([np.arange(0, A_HEAD_DIM, 2), np.arange(1, A_HEAD_DIM, 2)])
    idx = (np.arange(heads)[:, None] * A_HEAD_DIM + perm[None, :]).reshape(-1)
    return w[:, idx]


def kernel(x, c, ctx, c_ctx, w_mod, b_mod, g_pre, g_post, ab_w_in, ab_b_gate, ab_conv, ab_mnorm,
           ab_pool_w, ab_pool_scale, ab_w_out, c_w_in, c_sink, c_w_out):
    bsz, n, _ = x.shape
    n_ctx = ctx.shape[1]
    assert bsz < 16
    cs = jnp.zeros((16, D_MODEL), f32).at[:bsz].set(c).at[bsz].set(c_ctx)
    mod_all = _modulation(cs, w_mod, b_mod)
    cos, sin = _rope_tables(n)
    tm_x = 1024

    for l in range(DEPTH):
        j = l // 2
        last = l == DEPTH - 1
        mod_x = mod_all[l, :bsz].reshape(bsz, 1, 3 * D_MODEL)
        mod_c = mod_all[l, bsz:bsz + 1].reshape(1, 1, 3 * D_MODEL)
        gpre = g_pre[l].reshape(1, D_MODEL)
        gpost = g_post[l].reshape(1, D_MODEL)
        if l % 2 == 0:
            w = ab_w_in[j].astype(bf16)
            wg = jnp.pad(w[:, GATE_OFF:], ((0, 0), (0, GATE_PAD - N_GATE)))
            bg = jnp.pad(ab_b_gate[j], (0, GATE_PAD - N_GATE)).reshape(1, GATE_PAD)
            ux, gx = _in_ab(x, mod_x, gpre, w, wg, bg, tm_x)
            uc, gc = _in_ab(ctx, mod_c, gpre, w, wg, bg, n_ctx)
            ymx, ymc = _mlstm(ux, gx, uc, gc, ab_conv[j], ab_mnorm[j].reshape(1, M_WIDTH))
            wo = ab_w_out[j].astype(bf16)
            pw = ab_pool_w[j].astype(bf16)
            ps = ab_pool_scale[j].reshape(1, P_WIDTH)
            x = _out_ab(ymx, ux, x, mod_x, gpost, wo, pw, ps, 512)
            if not last:
                ctx = _out_ab(ymc, uc, ctx, mod_c, gpost, wo, pw, ps, n_ctx)
        else:
            w = c_w_in[j]
            w = jnp.concatenate([_split_pairs_columns(w[:, :A_WIDTH], A_HEADS),
                                 _split_pairs_columns(w[:, A_WIDTH:A_WIDTH + A_KV_WIDTH], A_KV_HEADS),
                                 w[:, A_WIDTH + A_KV_WIDTH:]], axis=1).astype(bf16)
            qx, kx, vx, zx = _in_c(x, mod_x, gpre, w, cos, sin, 512, True)
            qc, kc, vc, zc = _in_c(ctx, mod_c, gpre, w, cos[:n_ctx], sin[:n_ctx], n_ctx, False)
            wo = c_w_out[j].astype(bf16)
            ax = _attn(c_sink[j], qx, kx, vx, kc, vc, zx, True)
            x = _out_c(ax, x, mod_x, gpost, wo, tm_x)
            if not last:
                ac = _attn(c_sink[j], qc, kc, vc, kc, vc, zc, False)
                ctx = _out_c(ac, ctx, mod_c, gpost, wo, n_ctx)
    return x
```

```python
import functools

import jax
import jax.numpy as jnp
import numpy as np
from jax import lax
from jax.experimental import pallas as pl
from jax.experimental.pallas import tpu as pltpu

f32 = jnp.float32
bf16 = jnp.bfloat16

D_MODEL = 1024
DEPTH = 4
EPS = 1e-6
GRID_W = 64

M_HEADS = 4
M_HEAD_DIM = 256
M_WIDTH = 1024
M_CHUNK = 128
P_WINDOWS = (2, 4, 8, 16)
P_GROUP_DIM = 256
P_WIDTH = 1024
N_GATE = 16
GATE_OFF = 5 * M_WIDTH + 2 * P_WIDTH
GATE_PAD = 128

A_HEADS = 16
A_KV_HEADS = 4
A_GROUP = 4
A_HEAD_DIM = 64
A_WIDTH = 1024
A_KV_WIDTH = 256
A_WINDOW = 128
A_BLOCK = 128
A_BAND = A_BLOCK + 2 * A_WINDOW
ROPE_BASE = 10000.0

BF16_ROWS = 16
VMEM_LIMIT = 52 * 1024 * 1024


def _params(*sem):
    return pltpu.CompilerParams(dimension_semantics=sem, vmem_limit_bytes=VMEM_LIMIT)


def _silu(a):
    return a * jax.nn.sigmoid(a)


def _modulated_norm(x, gpre, mod):
    xn = x * lax.rsqrt(jnp.mean(x * x, -1, keepdims=True) + EPS) * gpre
    return xn * (1.0 + mod[:, D_MODEL:2 * D_MODEL]) + mod[:, :D_MODEL]


def _gated_residual(x, y, gpost, mod):
    r = y * lax.rsqrt(jnp.mean(y * y, -1, keepdims=True) + EPS) * gpost
    return x + mod[:, 2 * D_MODEL:] * r


def _mod_kernel(cs_ref, w_ref, b_ref, o_ref):
    a = _silu(cs_ref[...])
    o_ref[0] = jnp.dot(a, w_ref[0], preferred_element_type=f32,
                       precision=lax.Precision.HIGHEST) + b_ref[0]


def _modulation(cs, w_mod, b_mod):
    tn = 512
    return pl.pallas_call(
        _mod_kernel,
        grid=(DEPTH, 3 * D_MODEL // tn),
        in_specs=[pl.BlockSpec((16, D_MODEL), lambda l, j: (0, 0)),
                  pl.BlockSpec((1, D_MODEL, tn), lambda l, j: (l, 0, j)),
                  pl.BlockSpec((1, 1, tn), lambda l, j: (l, 0, j))],
        out_specs=pl.BlockSpec((1, 16, tn), lambda l, j: (l, 0, j)),
        out_shape=jax.ShapeDtypeStruct((DEPTH, 16, 3 * D_MODEL), f32),
        compiler_params=_params("parallel", "parallel"),
        name="modulation",
    )(cs, w_mod, b_mod.reshape(DEPTH, 1, 3 * D_MODEL))


def _in_ab_kernel(x_ref, mod_ref, gpre_ref, w_ref, wg_ref, bg_ref, u_ref, g_ref, h_scr):
    @pl.when(pl.program_id(2) == 0)
    def _():
        hb = _modulated_norm(x_ref[0], gpre_ref[...], mod_ref[0]).astype(bf16)
        h_scr[...] = hb
        g_ref[0] = jnp.dot(hb, wg_ref[...], preferred_element_type=f32) + bg_ref[...]

    u_ref[0] = jnp.dot(h_scr[...], w_ref[...], preferred_element_type=f32).astype(bf16)


def _in_ab(x, mod, gpre, w, wg, bg, tm):
    bsz, n, _ = x.shape
    tn = 1024
    return pl.pallas_call(
        _in_ab_kernel,
        grid=(bsz, n // tm, GATE_OFF // tn),
        in_specs=[pl.BlockSpec((1, tm, D_MODEL), lambda b, i, j: (b, i, 0)),
                  pl.BlockSpec((1, 1, 3 * D_MODEL), lambda b, i, j: (b % mod.shape[0], 0, 0)),
                  pl.BlockSpec((1, D_MODEL), lambda b, i, j: (0, 0)),
                  pl.BlockSpec((D_MODEL, tn), lambda b, i, j: (0, j)),
                  pl.BlockSpec((D_MODEL, GATE_PAD), lambda b, i, j: (0, 0)),
                  pl.BlockSpec((1, GATE_PAD), lambda b, i, j: (0, 0))],
        out_specs=[pl.BlockSpec((1, tm, tn), lambda b, i, j: (b, i, j)),
                   pl.BlockSpec((1, tm, GATE_PAD), lambda b, i, j: (b, i, 0))],
        out_shape=[jax.ShapeDtypeStruct((bsz, n, GATE_OFF), bf16),
                   jax.ShapeDtypeStruct((bsz, n, GATE_PAD), f32)],
        scratch_shapes=[pltpu.VMEM((tm, D_MODEL), bf16)],
        compiler_params=_params("parallel", "parallel", "arbitrary"),
        name="in_proj_ab",
    )(x, mod, gpre, w, wg, bg)


def _log_sigmoid(a):
    return jnp.minimum(a, 0.0) - jnp.log1p(jnp.exp(-jnp.abs(a)))


def _gate_rows(g_ref, r_scr, head):
    g = g_ref[0]
    n = g.shape[0]
    sub = lax.broadcasted_iota(jnp.int32, (8, GATE_PAD), 0)
    lane = lax.broadcasted_iota(jnp.int32, (8, GATE_PAD), 1)
    pick = jnp.where((lane == sub * M_HEADS + head) & (sub < 4), 1.0, 0.0).astype(bf16)
    rows = jnp.zeros((8, n), f32)
    rest = g
    for _ in range(3):
        piece = rest.astype(bf16)
        rest = rest - piece.astype(f32)
        rows = rows + lax.dot_general(pick, piece, (((1,), (1,)), ((), ())),
                                      preferred_element_type=f32)
    logf = _log_sigmoid(rows)
    pos = lax.broadcasted_iota(jnp.int32, (8, n), 1) % M_CHUNK
    pre, suf = logf, logf
    shift = 1
    while shift < M_CHUNK:
        pre = pre + jnp.where(pos >= shift, pltpu.roll(pre, shift, 1), 0.0)
        suf = suf + jnp.where(pos < M_CHUNK - shift, pltpu.roll(suf, n - shift, 1), 0.0)
        shift *= 2
    row = lax.broadcasted_iota(jnp.int32, (8, n), 0)
    r_scr[...] = jnp.where(row == 1, pre, jnp.where(row == 3, suf, rows))


def _row_to_col(row, eye):
    return jnp.sum(jnp.where(eye, row, 0.0), axis=-1, keepdims=True)


def _conv3(u_ref, w_ref, start, n_seq):
    static = isinstance(start, int)
    u = u_ref[0, pl.ds(start, M_CHUNK), :].astype(f32)
    if static:
        ps, ns = max(start - BF16_ROWS, 0), min(start + M_CHUNK, n_seq - BF16_ROWS)
    else:
        ps = pl.multiple_of(jnp.maximum(start - BF16_ROWS, 0), BF16_ROWS)
        ns = pl.multiple_of(jnp.minimum(start + M_CHUNK, n_seq - BF16_ROWS), BF16_ROWS)
    prev = u_ref[0, pl.ds(ps, BF16_ROWS), :].astype(f32)[BF16_ROWS - 1:BF16_ROWS]
    nxt = u_ref[0, pl.ds(ns, BF16_ROWS), :].astype(f32)[0:1]
    prev = jnp.where(start > 0, prev, 0.0)
    nxt = jnp.where(start + M_CHUNK < n_seq, nxt, 0.0)
    rid = lax.broadcasted_iota(jnp.int32, (M_CHUNK, 1), 0)
    up = jnp.where(rid == 0, prev, pltpu.roll(u, 1, 0))
    un = jnp.where(rid == M_CHUNK - 1, nxt, pltpu.roll(u, M_CHUNK - 1, 0))
    return w_ref[0:1, :] * up + w_ref[1:2, :] * u + w_ref[2:3, :] * un


def _mlstm_chunk(seg, consts, state, start, direction, first):
    q_ref, k_ref, v_ref, o_ref, z_ref, r_scr, acc, y_ref, n_seq = seg
    cq_ref, ck_ref, mn_ref = consts
    c_scr, n_scr, m_scr = state
    if not isinstance(start, int):
        start = pl.multiple_of(start, M_CHUNK)
    rows = pl.ds(start, M_CHUNK)

    q = _conv3(q_ref, cq_ref, start, n_seq)
    k = _conv3(k_ref, ck_ref, start, n_seq) * (M_HEAD_DIM ** -0.5)
    kt = k.T
    qb = q.astype(bf16)
    v = v_ref[0, rows, :]

    li = r_scr[2 * direction:2 * direction + 1, rows]
    bcum = r_scr[2 * direction + 1:2 * direction + 2, rows]
    ri = lax.broadcasted_iota(jnp.int32, (M_CHUNK, M_CHUNK), 0)
    ci = lax.broadcasted_iota(jnp.int32, (M_CHUNK, M_CHUNK), 1)
    eye = ri == ci
    causal = (ci <= ri) if direction == 0 else (ci >= ri)
    b_last = bcum[:, M_CHUNK - 1:M_CHUNK] if direction == 0 else bcum[:, 0:1]

    m_prev = m_scr[direction]
    c_mem = c_scr[direction]
    n_mem = n_scr[direction]

    b_col = _row_to_col(bcum, eye)
    dmat = jnp.where(causal, b_col + (li - bcum), -jnp.inf)
    inter = b_col + m_prev
    m_t = jnp.maximum(inter, jnp.max(dmat, axis=-1, keepdims=True))
    w_inter = jnp.exp(inter - m_t)
    s = jnp.dot(qb, kt.astype(bf16), preferred_element_type=f32) * jnp.exp(dmat - m_t)
    num = (w_inter * jnp.dot(qb, c_mem.astype(bf16), preferred_element_type=f32)
           + jnp.dot(s.astype(bf16), v, preferred_element_type=f32))
    den = (w_inter * jnp.sum(q * n_mem, axis=-1, keepdims=True)
           + jnp.sum(s, axis=-1, keepdims=True))
    h = num / jnp.maximum(jnp.abs(den), jnp.exp(-m_t))

    d_last = b_last - bcum + li
    m_new = jnp.maximum(b_last + m_prev, jnp.max(d_last, axis=-1, keepdims=True))
    w_row = jnp.exp(d_last - m_new)
    decay = jnp.exp(b_last + m_prev - m_new)
    c_scr[direction] = decay * c_mem + jnp.dot((kt * w_row).astype(bf16), v,
                                               preferred_element_type=f32)
    n_scr[direction] = decay * n_mem + jnp.sum(k * _row_to_col(w_row, eye), axis=0, keepdims=True)
    m_scr[direction] = m_new

    if first:
        acc[rows, :] = h
    else:
        hg = (acc[rows, :] + h) * jax.nn.sigmoid(o_ref[0, rows, :].astype(f32))
        hn = hg * lax.rsqrt(jnp.mean(hg * hg, -1, keepdims=True) + EPS) * mn_ref[...]
        y_ref[0, rows, :] = (hn * _silu(z_ref[0, rows, :].astype(f32))).astype(bf16)


def _mlstm_kernel(qx_ref, kx_ref, vx_ref, ox_ref, zx_ref, gx_ref,
                  qc_ref, kc_ref, vc_ref, oc_ref, zc_ref, gc_ref,
                  cq_ref, ck_ref, mn_ref, yx_ref, yc_ref,
                  rx_scr, rc_scr, accx, accc, c_scr, n_scr, m_scr):
    head = pl.program_id(1)
    nx, nc = qx_ref.shape[1], qc_ref.shape[1]
    _gate_rows(gx_ref, rx_scr, head)
    _gate_rows(gc_ref, rc_scr, head)
    c_scr[...] = jnp.zeros_like(c_scr)
    n_scr[...] = jnp.zeros_like(n_scr)
    m_scr[...] = jnp.zeros_like(m_scr)

    consts = (cq_ref, ck_ref, mn_ref)
    state = (c_scr, n_scr, m_scr)
    ctx = (qc_ref, kc_ref, vc_ref, oc_ref, zc_ref, rc_scr, accc, yc_ref, nc)
    lat = (qx_ref, kx_ref, vx_ref, ox_ref, zx_ref, rx_scr, accx, yx_ref, nx)
    step = functools.partial(_mlstm_chunk, consts=consts, state=state)

    ncc = nc // M_CHUNK
    assert ncc == 2
    step(ctx, start=0, direction=0, first=True)
    step(ctx, start=M_CHUNK, direction=1, first=True)
    step(ctx, start=M_CHUNK, direction=0, first=False)
    step(ctx, start=0, direction=1, first=False)

    nxc = nx // M_CHUNK

    def meet(first):
        def body(i, carry):
            step(lat, start=i * M_CHUNK, direction=0, first=first)
            step(lat, start=(nxc - 1 - i) * M_CHUNK, direction=1, first=first)
            return carry
        return body

    lax.fori_loop(0, nxc // 2, meet(True), 0)
    lax.fori_loop(nxc // 2, nxc, meet(False), 0)


def _mlstm(ux, gx, uc, gc, conv_w, mnorm):
    bsz, nx, _ = ux.shape
    nc = uc.shape[1]
    hd = M_HEAD_DIM

    def sect(n, k):
        return pl.BlockSpec((1, n, hd), lambda b, h, k=k: (b, 0, k * M_HEADS + h))

    def gates(n):
        return pl.BlockSpec((1, n, GATE_PAD), lambda b, h: (b, 0, 0))

    in_specs = ([sect(nx, k) for k in range(5)] + [gates(nx)]
                + [sect(nc, k) for k in range(5)] + [gates(nc)]
                + [pl.BlockSpec((3, hd), lambda b, h: (0, h)),
                   pl.BlockSpec((3, hd), lambda b, h: (0, M_HEADS + h)),
                   pl.BlockSpec((1, hd), lambda b, h: (0, h))])
    out_specs = [pl.BlockSpec((1, nx, hd), lambda b, h: (b, 0, h)),
                 pl.BlockSpec((1, nc, hd), lambda b, h: (b, 0, h))]
    return pl.pallas_call(
        _mlstm_kernel,
        grid=(bsz, M_HEADS),
        in_specs=in_specs,
        out_specs=out_specs,
        out_shape=[jax.ShapeDtypeStruct((bsz, nx, M_WIDTH), bf16),
                   jax.ShapeDtypeStruct((bsz, nc, M_WIDTH), bf16)],
        scratch_shapes=[pltpu.VMEM((8, nx), f32), pltpu.VMEM((8, nc), f32),
                        pltpu.VMEM((nx, hd), f32), pltpu.VMEM((nc, hd), f32),
                        pltpu.VMEM((2, hd, hd), f32), pltpu.VMEM((2, 1, hd), f32),
                        pltpu.VMEM((2, 1, 1), f32)],
        compiler_params=_params("parallel", "parallel"),
        name="mlstm",
    )(ux, ux, ux, ux, ux, gx, uc, uc, uc, uc, uc, gc, conv_w, conv_w, mnorm)


P_HALO = 16


def _out_ab_kernel(ym_ref, xp_ref, xpp_ref, xpn_ref, zp_ref, x_ref, mod_ref, gpost_ref,
                   wo_ref, pw_ref, ps_ref, o_ref, ext_scr):
    i, nt = pl.program_id(1), pl.num_programs(1)
    tm = ym_ref.shape[1]
    ext_scr[0:P_HALO, :] = jnp.where(i > 0, xpp_ref[0].astype(f32), 0.0)
    ext_scr[P_HALO:P_HALO + tm, :] = xp_ref[0].astype(f32)
    ext_scr[P_HALO + tm:, :] = jnp.where(i < nt - 1, xpn_ref[0].astype(f32), 0.0)

    t = i * tm + lax.broadcasted_iota(jnp.int32, (tm, 1), 0)
    n = nt * tm
    y = jnp.dot(ym_ref[0], wo_ref[0:M_WIDTH, :], preferred_element_type=f32)
    for g, win in enumerate(P_WINDOWS):
        half = win // 2
        cols = slice(g * P_GROUP_DIM, (g + 1) * P_GROUP_DIM)
        total = ext_scr[P_HALO - half:P_HALO - half + tm, cols]
        for d in range(-half + 1, half):
            total = total + ext_scr[P_HALO + d:P_HALO + d + tm, cols]
        count = (jnp.minimum(t + half, n) - jnp.maximum(t - half, 0)).astype(f32)
        pooled = total / count - ext_scr[P_HALO:P_HALO + tm, cols]
        mixed = jnp.dot(pooled.astype(bf16), pw_ref[g], preferred_element_type=f32) * ps_ref[:, cols]
        yp = (mixed * _silu(zp_ref[0, :, cols].astype(f32))).astype(bf16)
        y = y + jnp.dot(yp, wo_ref[M_WIDTH + g * P_GROUP_DIM:M_WIDTH + (g + 1) * P_GROUP_DIM, :],
                        preferred_element_type=f32)
    o_ref[0] = _gated_residual(x_ref[0], y, gpost_ref[...], mod_ref[0])


def _out_ab(ym, u, x, mod, gpost, wo, pw, ps, tm):
    bsz, n, _ = x.shape
    nh = n // P_HALO
    th = tm // P_HALO
    xp_col, zp_col = 5, 6
    return pl.pallas_call(
        _out_ab_kernel,
        grid=(bsz, n // tm),
        in_specs=[pl.BlockSpec((1, tm, M_WIDTH), lambda b, i: (b, i, 0)),
                  pl.BlockSpec((1, tm, P_WIDTH), lambda b, i: (b, i, xp_col)),
                  pl.BlockSpec((1, P_HALO, P_WIDTH), lambda b, i: (b, jnp.maximum(i * th - 1, 0), xp_col)),
                  pl.BlockSpec((1, P_HALO, P_WIDTH), lambda b, i: (b, jnp.minimum((i + 1) * th, nh - 1), xp_col)),
                  pl.BlockSpec((1, tm, P_WIDTH), lambda b, i: (b, i, zp_col)),
                  pl.BlockSpec((1, tm, D_MODEL), lambda b, i: (b, i, 0)),
                  pl.BlockSpec((1, 1, 3 * D_MODEL), lambda b, i: (b % mod.shape[0], 0, 0)),
                  pl.BlockSpec((1, D_MODEL), lambda b, i: (0, 0)),
                  pl.BlockSpec((M_WIDTH + P_WIDTH, D_MODEL), lambda b, i: (0, 0)),
                  pl.BlockSpec((len(P_WINDOWS), P_GROUP_DIM, P_GROUP_DIM), lambda b, i: (0, 0, 0)),
                  pl.BlockSpec((1, P_WIDTH), lambda b, i: (0, 0))],
        out_specs=pl.BlockSpec((1, tm, D_MODEL), lambda b, i: (b, i, 0)),
        out_shape=jax.ShapeDtypeStruct((bsz, n, D_MODEL), f32),
        scratch_shapes=[pltpu.VMEM((tm + 2 * P_HALO, P_WIDTH), f32)],
        compiler_params=_params("parallel", "parallel"),
        name="out_proj_ab",
    )(ym, u, u, u, u, x, mod, gpost, wo, pw, ps)


def _rope(a, cos, sin):
    lane = lax.broadcasted_iota(jnp.int32, a.shape, 1)
    half = A_HEAD_DIM // 2
    partner = jnp.where(lane % A_HEAD_DIM < half,
                        pltpu.roll(a, a.shape[1] - half, 1), pltpu.roll(a, half, 1))
    return a * cos + partner * sin


NT_DIMS = (((1,), (1,)), ((), ()))
LOG2E = 1.4426950408889634
C_ROWS = 512


def _in_c_kernel(x_ref, mod_ref, gpre_ref, wt_ref, wk_ref, cost_ref, sint_ref, cos_ref, sin_ref,
                 qt_ref, k_ref, vt_ref, zt_ref, *, rope):
    hb = _modulated_norm(x_ref[0], gpre_ref[...], mod_ref[0]).astype(bf16)

    def channel_major(lo, rows):
        return lax.dot_general(wt_ref[lo:lo + rows, :], hb, NT_DIMS, preferred_element_type=f32)

    half = A_HEAD_DIM // 2
    q_scale = (A_HEAD_DIM ** -0.5) * LOG2E
    for lo in range(0, A_WIDTH, C_ROWS):
        a = channel_major(lo, C_ROWS)
        for h in range(C_ROWS // A_HEAD_DIM):
            x1 = a[h * A_HEAD_DIM:h * A_HEAD_DIM + half]
            x2 = a[h * A_HEAD_DIM + half:(h + 1) * A_HEAD_DIM]
            if rope:
                c, s = cost_ref[...], sint_ref[...]
                x1, x2 = x1 * c - x2 * s, x1 * s + x2 * c
            r0 = lo + h * A_HEAD_DIM
            qt_ref[0, r0:r0 + half, :] = (x1 * q_scale).astype(bf16)
            qt_ref[0, r0 + half:r0 + A_HEAD_DIM, :] = (x2 * q_scale).astype(bf16)
    vt_ref[0] = channel_major(A_WIDTH, A_KV_WIDTH).astype(bf16)
    for lo in range(0, A_WIDTH, C_ROWS):
        zt_ref[0, lo:lo + C_ROWS, :] = channel_major(A_WIDTH + A_KV_WIDTH + lo, C_ROWS).astype(bf16)
    for j in range(A_KV_WIDTH // 128):
        kk = jnp.dot(hb, wk_ref[:, j * 128:(j + 1) * 128], preferred_element_type=f32)
        if rope:
            kk = _rope(kk, cos_ref[...], sin_ref[...])
        k_ref[0, :, j * 128:(j + 1) * 128] = kk.astype(bf16)


def _in_c(x, mod, gpre, wt, wk, cost, sint, cos, sin, tm, rope):
    bsz, n, _ = x.shape
    half = A_HEAD_DIM // 2
    return pl.pallas_call(
        functools.partial(_in_c_kernel, rope=rope),
        grid=(bsz, n // tm),
        in_specs=[pl.BlockSpec((1, tm, D_MODEL), lambda b, i: (b, i, 0)),
                  pl.BlockSpec((1, 1, 3 * D_MODEL), lambda b, i: (b % mod.shape[0], 0, 0)),
                  pl.BlockSpec((1, D_MODEL), lambda b, i: (0, 0)),
                  pl.BlockSpec(wt.shape, lambda b, i: (0, 0)),
                  pl.BlockSpec(wk.shape, lambda b, i: (0, 0)),
                  pl.BlockSpec((half, tm), lambda b, i: (0, i)),
                  pl.BlockSpec((half, tm), lambda b, i: (0, i)),
                  pl.BlockSpec((tm, 128), lambda b, i: (i, 0)),
                  pl.BlockSpec((tm, 128), lambda b, i: (i, 0))],
        out_specs=[pl.BlockSpec((1, A_WIDTH, tm), lambda b, i: (b, 0, i)),
                   pl.BlockSpec((1, tm, A_KV_WIDTH), lambda b, i: (b, i, 0)),
                   pl.BlockSpec((1, A_KV_WIDTH, tm), lambda b, i: (b, 0, i)),
                   pl.BlockSpec((1, A_WIDTH, tm), lambda b, i: (b, 0, i))],
        out_shape=[jax.ShapeDtypeStruct((bsz, A_WIDTH, n), bf16),
                   jax.ShapeDtypeStruct((bsz, n, A_KV_WIDTH), bf16),
                   jax.ShapeDtypeStruct((bsz, A_KV_WIDTH, n), bf16),
                   jax.ShapeDtypeStruct((bsz, A_WIDTH, n), bf16)],
        compiler_params=_params("parallel", "parallel"),
        name="in_proj_c",
    )(x, mod, gpre, wt, wk, cost, sint, cos, sin)


def _attn_kernel(sink_ref, qt_ref, k_ref, vt_ref, kc_ref, vct_ref, zt_ref, ogt_ref, s_scr, *, banded):
    nc = kc_ref.shape[1]
    if banded:
        n = k_ref.shape[1]
        start = pl.program_id(1) * A_BLOCK
        bs = pl.multiple_of(jnp.clip(start - A_WINDOW, 0, n - A_BAND), A_BLOCK)
        kj = bs + lax.broadcasted_iota(jnp.int32, (A_BAND, A_BLOCK), 0)
        qi = start + lax.broadcasted_iota(jnp.int32, (A_BAND, A_BLOCK), 1)
        bias = jnp.where(jnp.abs(qi - kj) <= A_WINDOW, 0.0, -jnp.inf)
        bias = jnp.concatenate([bias] * A_GROUP, axis=1)
    for g in range(A_KV_HEADS):
        heads = range(g * A_GROUP, (g + 1) * A_GROUP)
        pair = slice(128 * (g // 2), 128 * (g // 2) + 128)
        kvrows = slice(g * A_HEAD_DIM, (g + 1) * A_HEAD_DIM)
        qg = jnp.concatenate([qt_ref[0, h * A_HEAD_DIM:(h + 1) * A_HEAD_DIM, :] for h in heads], axis=1)
        zero = jnp.zeros_like(qg)
        qpad = jnp.concatenate([qg, zero] if g % 2 == 0 else [zero, qg], axis=0)
        s_scr[g, 0:nc, :] = jnp.dot(kc_ref[0, :, pair], qpad, preferred_element_type=f32)
        if banded:
            s_scr[g, nc:, :] = jnp.dot(k_ref[0, pl.ds(bs, A_BAND), pair], qpad,
                                       preferred_element_type=f32) + bias
        sk = jnp.concatenate([jnp.full((1, A_BLOCK), sink_ref[h] * LOG2E, f32) for h in heads], axis=1)
        s = s_scr[g]
        m = jnp.maximum(jnp.max(s, axis=0, keepdims=True), sk)
        p = jnp.exp2(s - m)
        inv = 1.0 / (jnp.sum(p, axis=0, keepdims=True) + jnp.exp2(sk - m))
        pb = p.astype(bf16)
        ot = jnp.dot(vct_ref[0, kvrows, :], pb[0:nc], preferred_element_type=f32)
        if banded:
            ot = ot + jnp.dot(vt_ref[0, kvrows, pl.ds(bs, A_BAND)], pb[nc:], preferred_element_type=f32)
        ot = ot * inv
        for a, h in enumerate(heads):
            rows = slice(h * A_HEAD_DIM, (h + 1) * A_HEAD_DIM)
            gate = _silu(zt_ref[0, rows, :].astype(f32))
            ogt_ref[0, rows, :] = (ot[:, a * A_BLOCK:(a + 1) * A_BLOCK] * gate).astype(bf16)


def _attn(sink, qt, k, vt, kc, vct, zt, banded):
    bsz, _, nq = qt.shape
    nk, nc = k.shape[1], kc.shape[1]
    nkeys = nc + (A_BAND if banded else 0)
    return pl.pallas_call(
        functools.partial(_attn_kernel, banded=banded),
        grid=(bsz, nq // A_BLOCK),
        in_specs=[pl.BlockSpec(memory_space=pltpu.SMEM),
                  pl.BlockSpec((1, A_WIDTH, A_BLOCK), lambda b, i: (b, 0, i)),
                  pl.BlockSpec((1, nk, A_KV_WIDTH), lambda b, i: (b, 0, 0)),
                  pl.BlockSpec((1, A_KV_WIDTH, nk), lambda b, i: (b, 0, 0)),
                  pl.BlockSpec((1, nc, A_KV_WIDTH), lambda b, i: (b, 0, 0)),
                  pl.BlockSpec((1, A_KV_WIDTH, nc), lambda b, i: (b, 0, 0)),
                  pl.BlockSpec((1, A_WIDTH, A_BLOCK), lambda b, i: (b, 0, i))],
        out_specs=pl.BlockSpec((1, A_WIDTH, A_BLOCK), lambda b, i: (b, 0, i)),
        out_shape=jax.ShapeDtypeStruct((bsz, A_WIDTH, nq), bf16),
        scratch_shapes=[pltpu.VMEM((A_KV_HEADS, nkeys, A_GROUP * A_BLOCK), f32)],
        compiler_params=_params("parallel", "parallel"),
        name="attention_banded" if banded else "attention_context",
    )(sink, qt, k, vt, kc, vct, zt)


def _out_c_kernel(at_ref, x_ref, mod_ref, gpost_ref, wo_ref, o_ref):
    y = jnp.dot(at_ref[0].T, wo_ref[...], preferred_element_type=f32)
    o_ref[0] = _gated_residual(x_ref[0], y, gpost_ref[...], mod_ref[0])


def _out_c(at, x, mod, gpost, wo, tm):
    bsz, n, _ = x.shape
    return pl.pallas_call(
        _out_c_kernel,
        grid=(bsz, n // tm),
        in_specs=[pl.BlockSpec((1, A_WIDTH, tm), lambda b, i: (b, 0, i)),
                  pl.BlockSpec((1, tm, D_MODEL), lambda b, i: (b, i, 0)),
                  pl.BlockSpec((1, 1, 3 * D_MODEL), lambda b, i: (b % mod.shape[0], 0, 0)),
                  pl.BlockSpec((1, D_MODEL), lambda b, i: (0, 0)),
                  pl.BlockSpec((A_WIDTH, D_MODEL), lambda b, i: (0, 0))],
        out_specs=pl.BlockSpec((1, tm, D_MODEL), lambda b, i: (b, i, 0)),
        out_shape=jax.ShapeDtypeStruct((bsz, n, D_MODEL), f32),
        compiler_params=_params("parallel", "parallel"),
        name="out_proj_c",
    )(at, x, mod, gpost, wo)


def _rope_tables(n):
    rows = n // GRID_W
    row = jnp.repeat(jnp.arange(rows), GRID_W).astype(f32)
    col = jnp.tile(jnp.arange(GRID_W), rows).astype(f32)
    n_freq = A_HEAD_DIM // 4
    inv = ROPE_BASE ** (-jnp.arange(n_freq, dtype=f32) / n_freq)
    ang = jnp.concatenate([row[:, None] * inv, col[:, None] * inv], -1)
    cos, sin = jnp.cos(ang), jnp.sin(ang)
    cos_t = jnp.tile(jnp.concatenate([cos, cos], -1), (1, 128 // A_HEAD_DIM))
    sin_t = jnp.tile(jnp.concatenate([-sin, sin], -1), (1, 128 // A_HEAD_DIM))
    return cos.T, sin.T, cos_t, sin_t


def _split_pairs_columns(w, heads):
    perm = np.concatenate([np.arange(0, A_HEAD_DIM, 2), np.arange(1, A_HEAD_DIM, 2)])
    idx = (np.arange(heads)[:, None] * A_HEAD_DIM + perm[None, :]).reshape(-1)
    return w[:, idx]


def kernel(x, c, ctx, c_ctx, w_mod, b_mod, g_pre, g_post, ab_w_in, ab_b_gate, ab_conv, ab_mnorm,
           ab_pool_w, ab_pool_scale, ab_w_out, c_w_in, c_sink, c_w_out):
    bsz, n, _ = x.shape
    n_ctx = ctx.shape[1]
    assert bsz < 16
    cs = jnp.zeros((16, D_MODEL), f32).at[:bsz].set(c).at[bsz].set(c_ctx)
    mod_all = _modulation(cs, w_mod, b_mod)
    cost, sint, cos, sin = _rope_tables(n)
    tm_x = 1024

    for l in range(DEPTH):
        j = l // 2
        last = l == DEPTH - 1
        mod_x = mod_all[l, :bsz].reshape(bsz, 1, 3 * D_MODEL)
        mod_c = mod_all[l, bsz:bsz + 1].reshape(1, 1, 3 * D_MODEL)
        gpre = g_pre[l].reshape(1, D_MODEL)
        gpost = g_post[l].reshape(1, D_MODEL)
        if l % 2 == 0:
            w = ab_w_in[j].astype(bf16)
            wg = jnp.pad(w[:, GATE_OFF:], ((0, 0), (0, GATE_PAD - N_GATE)))
            bg = jnp.pad(ab_b_gate[j], (0, GATE_PAD - N_GATE)).reshape(1, GATE_PAD)
            ux, gx = _in_ab(x, mod_x, gpre, w, wg, bg, tm_x)
            uc, gc = _in_ab(ctx, mod_c, gpre, w, wg, bg, n_ctx)
            ymx, ymc = _mlstm(ux, gx, uc, gc, ab_conv[j], ab_mnorm[j].reshape(1, M_WIDTH))
            wo = ab_w_out[j].astype(bf16)
            pw = ab_pool_w[j].astype(bf16)
            ps = ab_pool_scale[j].reshape(1, P_WIDTH)
            x = _out_ab(ymx, ux, x, mod_x, gpost, wo, pw, ps, 512)
            if not last:
                ctx = _out_ab(ymc, uc, ctx, mod_c, gpost, wo, pw, ps, n_ctx)
        else:
            w = c_w_in[j].astype(bf16)
            wt = jnp.concatenate([_split_pairs_columns(w[:, :A_WIDTH], A_HEADS),
                                  w[:, A_WIDTH + A_KV_WIDTH:]], axis=1).T
            wk = _split_pairs_columns(w[:, A_WIDTH:A_WIDTH + A_KV_WIDTH], A_KV_HEADS)
            qx, kx, vx, zx = _in_c(x, mod_x, gpre, wt, wk, cost, sint, cos, sin, 512, True)
            qc, kc, vc, zc = _in_c(ctx, mod_c, gpre, wt, wk, cost[:, :n_ctx], sint[:, :n_ctx],
                                   cos[:n_ctx], sin[:n_ctx], n_ctx, False)
            wo = c_w_out[j].astype(bf16)
            ax = _attn(c_sink[j], qx, kx, vx, kc, vc, zx, True)
            x = _out_c(ax, x, mod_x, gpost, wo, tm_x)
            if not last:
                ac = _attn(c_sink[j], qc, kc, vc, kc, vc, zc, False)
                ctx = _out_c(ac, ctx, mod_c, gpost, wo, n_ctx)
    return x
```

```python
import functools

import jax
import jax.numpy as jnp
import numpy as np
from jax import lax
from jax.experimental import pallas as pl
from jax.experimental.pallas import tpu as pltpu

f32 = jnp.float32
bf16 = jnp.bfloat16

D_MODEL = 1024
DEPTH = 4
EPS = 1e-6
GRID_W = 64

M_HEADS = 4
M_HEAD_DIM = 256
M_WIDTH = 1024
M_CHUNK = 128
P_WINDOWS = (2, 4, 8, 16)
P_GROUP_DIM = 256
P_WIDTH = 1024
N_GATE = 16
GATE_OFF = 5 * M_WIDTH + 2 * P_WIDTH
GATE_PAD = 128

A_HEADS = 16
A_KV_HEADS = 4
A_GROUP = 4
A_HEAD_DIM = 64
A_WIDTH = 1024
A_KV_WIDTH = 256
A_WINDOW = 128
A_BLOCK = 128
A_BAND = A_BLOCK + 2 * A_WINDOW
ROPE_BASE = 10000.0

BF16_ROWS = 16
VMEM_LIMIT = 52 * 1024 * 1024


def _params(*sem):
    return pltpu.CompilerParams(dimension_semantics=sem, vmem_limit_bytes=VMEM_LIMIT)


def _silu(a):
    return a * jax.nn.sigmoid(a)


def _modulated_norm(x, gpre, mod):
    xn = x * lax.rsqrt(jnp.mean(x * x, -1, keepdims=True) + EPS) * gpre
    return xn * (1.0 + mod[:, D_MODEL:2 * D_MODEL]) + mod[:, :D_MODEL]


def _gated_residual(x, y, gpost, mod):
    r = y * lax.rsqrt(jnp.mean(y * y, -1, keepdims=True) + EPS) * gpost
    return x + mod[:, 2 * D_MODEL:] * r


def _mod_kernel(cs_ref, w_ref, b_ref, o_ref):
    a = _silu(cs_ref[...])
    o_ref[0] = jnp.dot(a, w_ref[0], preferred_element_type=f32,
                       precision=lax.Precision.HIGHEST) + b_ref[0]


def _modulation(cs, w_mod, b_mod):
    tn = 512
    return pl.pallas_call(
        _mod_kernel,
        grid=(DEPTH, 3 * D_MODEL // tn),
        in_specs=[pl.BlockSpec((16, D_MODEL), lambda l, j: (0, 0)),
                  pl.BlockSpec((1, D_MODEL, tn), lambda l, j: (l, 0, j)),
                  pl.BlockSpec((1, 1, tn), lambda l, j: (l, 0, j))],
        out_specs=pl.BlockSpec((1, 16, tn), lambda l, j: (l, 0, j)),
        out_shape=jax.ShapeDtypeStruct((DEPTH, 16, 3 * D_MODEL), f32),
        compiler_params=_params("parallel", "parallel"),
        name="modulation",
    )(cs, w_mod, b_mod.reshape(DEPTH, 1, 3 * D_MODEL))


NT_DIMS = (((1,), (1,)), ((), ()))
C_ROWS = 512
HALO = 16


def _in_tok_kernel(x_ref, xprev_ref, xnext_ref, mod_ref, gpre_ref, w_ref, wg_ref, bg_ref, cw_ref,
                   q_ref, k_ref, xp_ref, zp_ref, g_ref, h_scr, r_scr):
    i, nt, j = pl.program_id(1), pl.num_programs(1), pl.program_id(2)
    tm = x_ref.shape[1]

    @pl.when(j == 0)
    def _():
        def normed(a):
            return _modulated_norm(a, gpre_ref[...], mod_ref[0]).astype(bf16)

        h_scr[0:HALO, :] = normed(xprev_ref[0])
        h_scr[HALO:HALO + tm, :] = normed(x_ref[0])
        h_scr[HALO + tm:, :] = normed(xnext_ref[0])
        g_ref[0] = jnp.dot(h_scr[HALO:HALO + tm, :], wg_ref[...], preferred_element_type=f32) + bg_ref[...]
        r_scr[...] = jnp.dot(h_scr[...], w_ref[...], preferred_element_type=f32)

        @pl.when(i == 0)
        def _():
            r_scr[0:HALO, :] = jnp.zeros((HALO, r_scr.shape[1]), f32)

        @pl.when(i == nt - 1)
        def _():
            r_scr[HALO + tm:, :] = jnp.zeros((HALO, r_scr.shape[1]), f32)

        for half, out_ref in enumerate((q_ref, k_ref)):
            cols = slice(half * M_WIDTH, (half + 1) * M_WIDTH)
            out_ref[0] = (cw_ref[0:1, cols] * r_scr[HALO - 1:HALO - 1 + tm, cols]
                          + cw_ref[1:2, cols] * r_scr[HALO:HALO + tm, cols]
                          + cw_ref[2:3, cols] * r_scr[HALO + 1:HALO + 1 + tm, cols]).astype(bf16)

    @pl.when(j == 1)
    def _():
        r = jnp.dot(h_scr[HALO:HALO + tm, :], w_ref[...], preferred_element_type=f32)
        xp_ref[0] = r[:, :P_WIDTH].astype(bf16)
        zp_ref[0] = r[:, P_WIDTH:].astype(bf16)


def _in_tok(x, mod, gpre, w, wg, bg, cw, tm):
    bsz, n, _ = x.shape
    th, nh = tm // HALO, n // HALO
    tok = jax.ShapeDtypeStruct((bsz, n, M_WIDTH), bf16)
    tok_spec = pl.BlockSpec((1, tm, M_WIDTH), lambda b, i, j: (b, i, 0))
    return pl.pallas_call(
        _in_tok_kernel,
        grid=(bsz, n // tm, 2),
        in_specs=[pl.BlockSpec((1, tm, D_MODEL), lambda b, i, j: (b, i, 0)),
                  pl.BlockSpec((1, HALO, D_MODEL), lambda b, i, j: (b, jnp.maximum(i * th - 1, 0), 0)),
                  pl.BlockSpec((1, HALO, D_MODEL), lambda b, i, j: (b, jnp.minimum((i + 1) * th, nh - 1), 0)),
                  pl.BlockSpec((1, 1, 3 * D_MODEL), lambda b, i, j: (b % mod.shape[0], 0, 0)),
                  pl.BlockSpec((1, D_MODEL), lambda b, i, j: (0, 0)),
                  pl.BlockSpec((D_MODEL, 2 * M_WIDTH), lambda b, i, j: (0, j)),
                  pl.BlockSpec((D_MODEL, GATE_PAD), lambda b, i, j: (0, 0)),
                  pl.BlockSpec((1, GATE_PAD), lambda b, i, j: (0, 0)),
                  pl.BlockSpec((3, 2 * M_WIDTH), lambda b, i, j: (0, 0))],
        out_specs=[tok_spec, tok_spec, tok_spec, tok_spec,
                   pl.BlockSpec((1, tm, GATE_PAD), lambda b, i, j: (b, i, 0))],
        out_shape=[tok, tok, tok, tok, jax.ShapeDtypeStruct((bsz, n, GATE_PAD), f32)],
        scratch_shapes=[pltpu.VMEM((tm + 2 * HALO, D_MODEL), bf16),
                        pltpu.VMEM((tm + 2 * HALO, 2 * M_WIDTH), f32)],
        compiler_params=_params("parallel", "parallel", "arbitrary"),
        name="in_proj_ab_tok",
    )(x, x, x, mod, gpre, w, wg, bg, cw)


def _in_ch_kernel(x_ref, mod_ref, gpre_ref, wt_ref, vt_ref, ot_ref, zt_ref):
    hb = _modulated_norm(x_ref[0], gpre_ref[...], mod_ref[0]).astype(bf16)
    for idx, out_ref in enumerate((vt_ref, ot_ref, zt_ref)):
        for lo in range(0, M_WIDTH, C_ROWS):
            rows = wt_ref[idx * M_WIDTH + lo:idx * M_WIDTH + lo + C_ROWS, :]
            out_ref[0, lo:lo + C_ROWS, :] = lax.dot_general(
                rows, hb, NT_DIMS, preferred_element_type=f32).astype(bf16)


def _in_ch(x, mod, gpre, wt, tm):
    bsz, n, _ = x.shape
    ch = jax.ShapeDtypeStruct((bsz, M_WIDTH, n), bf16)
    ch_spec = pl.BlockSpec((1, M_WIDTH, tm), lambda b, i: (b, 0, i))
    return pl.pallas_call(
        _in_ch_kernel,
        grid=(bsz, n // tm),
        in_specs=[pl.BlockSpec((1, tm, D_MODEL), lambda b, i: (b, i, 0)),
                  pl.BlockSpec((1, 1, 3 * D_MODEL), lambda b, i: (b % mod.shape[0], 0, 0)),
                  pl.BlockSpec((1, D_MODEL), lambda b, i: (0, 0)),
                  pl.BlockSpec(wt.shape, lambda b, i: (0, 0))],
        out_specs=[ch_spec, ch_spec, ch_spec],
        out_shape=[ch, ch, ch],
        compiler_params=_params("parallel", "parallel"),
        name="in_proj_ab_ch",
    )(x, mod, gpre, wt)


def _log_sigmoid(a):
    return jnp.minimum(a, 0.0) - jnp.log1p(jnp.exp(-jnp.abs(a)))


def _gate_rows(g_ref, r_scr, head):
    g = g_ref[0]
    n = g.shape[0]
    sub = lax.broadcasted_iota(jnp.int32, (8, GATE_PAD), 0)
    lane = lax.broadcasted_iota(jnp.int32, (8, GATE_PAD), 1)
    pick = jnp.where((lane == sub * M_HEADS + head) & (sub < 4), 1.0, 0.0).astype(bf16)
    rows = jnp.zeros((8, n), f32)
    rest = g
    for _ in range(3):
        piece = rest.astype(bf16)
        rest = rest - piece.astype(f32)
        rows = rows + lax.dot_general(pick, piece, (((1,), (1,)), ((), ())),
                                      preferred_element_type=f32)
    logf = _log_sigmoid(rows)
    pos = lax.broadcasted_iota(jnp.int32, (8, n), 1) % M_CHUNK
    pre, suf = logf, logf
    shift = 1
    while shift < M_CHUNK:
        pre = pre + jnp.where(pos >= shift, pltpu.roll(pre, shift, 1), 0.0)
        suf = suf + jnp.where(pos < M_CHUNK - shift, pltpu.roll(suf, n - shift, 1), 0.0)
        shift *= 2
    row = lax.broadcasted_iota(jnp.int32, (8, n), 0)
    r_scr[...] = jnp.where(row == 1, pre, jnp.where(row == 3, suf, rows))


STATE_ROWS = M_HEAD_DIM + BF16_ROWS


def _mlstm_chunk(seg, consts, state, start, direction, first):
    q_ref, k_ref, vt_ref, ot_ref, zt_ref, r_scr, acc, yt_ref = seg
    (mn_ref,) = consts
    ct_scr, m_scr = state
    if not isinstance(start, int):
        start = pl.multiple_of(start, M_CHUNK)
    span = pl.ds(start, M_CHUNK)
    dk = M_HEAD_DIM

    q = q_ref[0, span, :]
    k = k_ref[0, span, :]
    vt = vt_ref[0, :, span]

    li = r_scr[2 * direction:2 * direction + 1, span]
    bcum = r_scr[2 * direction + 1:2 * direction + 2, span]
    si = lax.broadcasted_iota(jnp.int32, (M_CHUNK, M_CHUNK), 0)
    ti = lax.broadcasted_iota(jnp.int32, (M_CHUNK, M_CHUNK), 1)
    causal = (si <= ti) if direction == 0 else (si >= ti)
    b_last = bcum[:, M_CHUNK - 1:M_CHUNK] if direction == 0 else bcum[:, 0:1]

    m_prev = m_scr[direction]
    ct = ct_scr[direction]
    n_mem = ct[dk:dk + 1, :]

    key_side = jnp.transpose(jnp.broadcast_to(li - bcum, (M_CHUNK, M_CHUNK)))
    dmat = jnp.where(causal, bcum + key_side, -jnp.inf)
    inter = bcum + m_prev
    m_t = jnp.maximum(inter, jnp.max(dmat, axis=0, keepdims=True))
    w_inter = jnp.exp(inter - m_t)
    s = lax.dot_general(k, q, NT_DIMS, preferred_element_type=f32) * jnp.exp(dmat - m_t)

    tail = lax.broadcasted_iota(jnp.int32, (BF16_ROWS, dk), 0)
    n_hi = n_mem.astype(bf16).astype(f32)
    extra = jnp.where(tail == 0, n_hi, jnp.where(tail == 1, n_mem - n_hi, 0.0))
    lhs = jnp.concatenate([ct[:dk].astype(bf16), extra.astype(bf16)], axis=0)
    from_state = lax.dot_general(lhs, q, NT_DIMS, preferred_element_type=f32)
    num = w_inter * from_state[:dk] + jnp.dot(vt, s.astype(bf16), preferred_element_type=f32)
    den = (w_inter * (from_state[dk:dk + 1] + from_state[dk + 1:dk + 2])
           + jnp.sum(s, axis=0, keepdims=True))
    h = num * (1.0 / jnp.maximum(jnp.abs(den), jnp.exp(-m_t)))

    d_last = b_last - bcum + li
    m_new = jnp.maximum(b_last + m_prev, jnp.max(d_last, axis=-1, keepdims=True))
    w_row = jnp.exp(d_last - m_new)
    decay = jnp.exp(b_last + m_prev - m_new)
    w_hi = w_row.astype(bf16).astype(f32)
    w_extra = jnp.where(tail[:, :M_CHUNK] == 0, w_hi, jnp.where(tail[:, :M_CHUNK] == 1, w_row - w_hi, 0.0))
    va = jnp.concatenate([(vt.astype(f32) * w_row).astype(bf16), w_extra.astype(bf16)], axis=0)
    update = jnp.dot(va, k, preferred_element_type=f32)
    ct_scr[direction, 0:dk, :] = decay * ct[:dk] + update[:dk]
    n_new = decay * n_mem + update[dk:dk + 1] + update[dk + 1:dk + 2]
    ct_scr[direction, dk:, :] = jnp.where(tail == 0, n_new, 0.0)
    m_scr[direction] = m_new

    if first:
        acc[:, span] = h
    else:
        hg = (acc[:, span] + h) * jax.nn.sigmoid(ot_ref[0, :, span].astype(f32))
        hn = hg * lax.rsqrt(jnp.mean(hg * hg, axis=0, keepdims=True) + EPS) * mn_ref[...]
        yt_ref[0, :, span] = (hn * _silu(zt_ref[0, :, span].astype(f32))).astype(bf16)


def _mlstm_kernel(qx_ref, kx_ref, vx_ref, ox_ref, zx_ref, gx_ref,
                  qc_ref, kc_ref, vc_ref, oc_ref, zc_ref, gc_ref,
                  mn_ref, yx_ref, yc_ref,
                  rx_scr, rc_scr, accx, accc, ct_scr, m_scr):
    head = pl.program_id(1)
    nx, nc = qx_ref.shape[1], qc_ref.shape[1]
    _gate_rows(gx_ref, rx_scr, head)
    _gate_rows(gc_ref, rc_scr, head)
    ct_scr[...] = jnp.zeros_like(ct_scr)
    m_scr[...] = jnp.zeros_like(m_scr)

    consts = (mn_ref,)
    state = (ct_scr, m_scr)
    ctx = (qc_ref, kc_ref, vc_ref, oc_ref, zc_ref, rc_scr, accc, yc_ref)
    lat = (qx_ref, kx_ref, vx_ref, ox_ref, zx_ref, rx_scr, accx, yx_ref)
    step = functools.partial(_mlstm_chunk, consts=consts, state=state)

    ncc = nc // M_CHUNK
    assert ncc == 2
    step(ctx, start=0, direction=0, first=True)
    step(ctx, start=M_CHUNK, direction=1, first=True)
    step(ctx, start=M_CHUNK, direction=0, first=False)
    step(ctx, start=0, direction=1, first=False)

    nxc = nx // M_CHUNK

    def meet(first):
        def body(i, carry):
            step(lat, start=i * M_CHUNK, direction=0, first=first)
            step(lat, start=(nxc - 1 - i) * M_CHUNK, direction=1, first=first)
            return carry
        return body

    lax.fori_loop(0, nxc // 2, meet(True), 0)
    lax.fori_loop(nxc // 2, nxc, meet(False), 0)


def _mlstm(lat, ctx, mnorm_tab):
    bsz, nx, _ = lat[0].shape
    nc = ctx[0].shape[1]
    hd = M_HEAD_DIM

    def specs(n):
        tok = pl.BlockSpec((1, n, hd), lambda b, h: (b, 0, h))
        ch = pl.BlockSpec((1, hd, n), lambda b, h: (b, h, 0))
        return [tok, tok, ch, ch, ch, pl.BlockSpec((1, n, GATE_PAD), lambda b, h: (b, 0, 0))]

    return pl.pallas_call(
        _mlstm_kernel,
        grid=(bsz, M_HEADS),
        in_specs=specs(nx) + specs(nc) + [pl.BlockSpec((hd, 128), lambda b, h: (h, 0))],
        out_specs=[pl.BlockSpec((1, hd, nx), lambda b, h: (b, h, 0)),
                   pl.BlockSpec((1, hd, nc), lambda b, h: (b, h, 0))],
        out_shape=[jax.ShapeDtypeStruct((bsz, M_WIDTH, nx), bf16),
                   jax.ShapeDtypeStruct((bsz, M_WIDTH, nc), bf16)],
        scratch_shapes=[pltpu.VMEM((8, nx), f32), pltpu.VMEM((8, nc), f32),
                        pltpu.VMEM((hd, nx), f32), pltpu.VMEM((hd, nc), f32),
                        pltpu.VMEM((2, STATE_ROWS, hd), f32), pltpu.VMEM((2, 1, 1), f32)],
        compiler_params=_params("parallel", "parallel"),
        name="mlstm",
    )(*lat, *ctx, mnorm_tab)


P_HALO = 16


def _out_ab_kernel(ym_ref, xp_ref, xpp_ref, xpn_ref, zp_ref, x_ref, mod_ref, gpost_ref,
                   wo_ref, pw_ref, ps_ref, o_ref, ext_scr):
    i, nt = pl.program_id(1), pl.num_programs(1)
    tm = x_ref.shape[1]
    ext_scr[0:P_HALO, :] = jnp.where(i > 0, xpp_ref[0].astype(f32), 0.0)
    ext_scr[P_HALO:P_HALO + tm, :] = xp_ref[0].astype(f32)
    ext_scr[P_HALO + tm:, :] = jnp.where(i < nt - 1, xpn_ref[0].astype(f32), 0.0)

    t = i * tm + lax.broadcasted_iota(jnp.int32, (tm, 1), 0)
    n = nt * tm
    y = jnp.dot(ym_ref[0].T, wo_ref[0:M_WIDTH, :], preferred_element_type=f32)
    for g, win in enumerate(P_WINDOWS):
        half = win // 2
        cols = slice(g * P_GROUP_DIM, (g + 1) * P_GROUP_DIM)
        total = ext_scr[P_HALO - half:P_HALO - half + tm, cols]
        for d in range(-half + 1, half):
            total = total + ext_scr[P_HALO + d:P_HALO + d + tm, cols]
        count = (jnp.minimum(t + half, n) - jnp.maximum(t - half, 0)).astype(f32)
        pooled = total / count - ext_scr[P_HALO:P_HALO + tm, cols]
        mixed = jnp.dot(pooled.astype(bf16), pw_ref[g], preferred_element_type=f32) * ps_ref[:, cols]
        yp = (mixed * _silu(zp_ref[0, :, cols].astype(f32))).astype(bf16)
        y = y + jnp.dot(yp, wo_ref[M_WIDTH + g * P_GROUP_DIM:M_WIDTH + (g + 1) * P_GROUP_DIM, :],
                        preferred_element_type=f32)
    o_ref[0] = _gated_residual(x_ref[0], y, gpost_ref[...], mod_ref[0])


def _out_ab(ymt, xp, zp, x, mod, gpost, wo, pw, ps, tm):
    bsz, n, _ = x.shape
    nh = n // P_HALO
    th = tm // P_HALO
    return pl.pallas_call(
        _out_ab_kernel,
        grid=(bsz, n // tm),
        in_specs=[pl.BlockSpec((1, M_WIDTH, tm), lambda b, i: (b, 0, i)),
                  pl.BlockSpec((1, tm, P_WIDTH), lambda b, i: (b, i, 0)),
                  pl.BlockSpec((1, P_HALO, P_WIDTH), lambda b, i: (b, jnp.maximum(i * th - 1, 0), 0)),
                  pl.BlockSpec((1, P_HALO, P_WIDTH), lambda b, i: (b, jnp.minimum((i + 1) * th, nh - 1), 0)),
                  pl.BlockSpec((1, tm, P_WIDTH), lambda b, i: (b, i, 0)),
                  pl.BlockSpec((1, tm, D_MODEL), lambda b, i: (b, i, 0)),
                  pl.BlockSpec((1, 1, 3 * D_MODEL), lambda b, i: (b % mod.shape[0], 0, 0)),
                  pl.BlockSpec((1, D_MODEL), lambda b, i: (0, 0)),
                  pl.BlockSpec((M_WIDTH + P_WIDTH, D_MODEL), lambda b, i: (0, 0)),
                  pl.BlockSpec((len(P_WINDOWS), P_GROUP_DIM, P_GROUP_DIM), lambda b, i: (0, 0, 0)),
                  pl.BlockSpec((1, P_WIDTH), lambda b, i: (0, 0))],
        out_specs=pl.BlockSpec((1, tm, D_MODEL), lambda b, i: (b, i, 0)),
        out_shape=jax.ShapeDtypeStruct((bsz, n, D_MODEL), f32),
        scratch_shapes=[pltpu.VMEM((tm + 2 * P_HALO, P_WIDTH), f32)],
        compiler_params=_params("parallel", "parallel"),
        name="out_proj_ab",
    )(ymt, xp, xp, xp, zp, x, mod, gpost, wo, pw, ps)


def _rope(a, cos, sin):
    lane = lax.broadcasted_iota(jnp.int32, a.shape, 1)
    half = A_HEAD_DIM // 2
    partner = jnp.where(lane % A_HEAD_DIM < half,
                        pltpu.roll(a, a.shape[1] - half, 1), pltpu.roll(a, half, 1))
    return a * cos + partner * sin


LOG2E = 1.4426950408889634


def _in_c_kernel(x_ref, mod_ref, gpre_ref, wt_ref, wk_ref, cost_ref, sint_ref, cos_ref, sin_ref,
                 qt_ref, k_ref, vt_ref, zt_ref, *, rope):
    hb = _modulated_norm(x_ref[0], gpre_ref[...], mod_ref[0]).astype(bf16)

    def channel_major(lo, rows):
        return lax.dot_general(wt_ref[lo:lo + rows, :], hb, NT_DIMS, preferred_element_type=f32)

    half = A_HEAD_DIM // 2
    q_scale = (A_HEAD_DIM ** -0.5) * LOG2E
    for lo in range(0, A_WIDTH, C_ROWS):
        a = channel_major(lo, C_ROWS)
        for h in range(C_ROWS // A_HEAD_DIM):
            x1 = a[h * A_HEAD_DIM:h * A_HEAD_DIM + half]
            x2 = a[h * A_HEAD_DIM + half:(h + 1) * A_HEAD_DIM]
            if rope:
                c, s = cost_ref[...], sint_ref[...]
                x1, x2 = x1 * c - x2 * s, x1 * s + x2 * c
            r0 = lo + h * A_HEAD_DIM
            qt_ref[0, r0:r0 + half, :] = (x1 * q_scale).astype(bf16)
            qt_ref[0, r0 + half:r0 + A_HEAD_DIM, :] = (x2 * q_scale).astype(bf16)
    vt_ref[0] = channel_major(A_WIDTH, A_KV_WIDTH).astype(bf16)
    for lo in range(0, A_WIDTH, C_ROWS):
        zt_ref[0, lo:lo + C_ROWS, :] = channel_major(A_WIDTH + A_KV_WIDTH + lo, C_ROWS).astype(bf16)
    for j in range(A_KV_WIDTH // 128):
        kk = jnp.dot(hb, wk_ref[:, j * 128:(j + 1) * 128], preferred_element_type=f32)
        if rope:
            kk = _rope(kk, cos_ref[...], sin_ref[...])
        k_ref[0, :, j * 128:(j + 1) * 128] = kk.astype(bf16)


def _in_c(x, mod, gpre, wt, wk, cost, sint, cos, sin, tm, rope):
    bsz, n, _ = x.shape
    half = A_HEAD_DIM // 2
    return pl.pallas_call(
        functools.partial(_in_c_kernel, rope=rope),
        grid=(bsz, n // tm),
        in_specs=[pl.BlockSpec((1, tm, D_MODEL), lambda b, i: (b, i, 0)),
                  pl.BlockSpec((1, 1, 3 * D_MODEL), lambda b, i: (b % mod.shape[0], 0, 0)),
                  pl.BlockSpec((1, D_MODEL), lambda b, i: (0, 0)),
                  pl.BlockSpec(wt.shape, lambda b, i: (0, 0)),
                  pl.BlockSpec(wk.shape, lambda b, i: (0, 0)),
                  pl.BlockSpec((half, tm), lambda b, i: (0, i)),
                  pl.BlockSpec((half, tm), lambda b, i: (0, i)),
                  pl.BlockSpec((tm, 128), lambda b, i: (i, 0)),
                  pl.BlockSpec((tm, 128), lambda b, i: (i, 0))],
        out_specs=[pl.BlockSpec((1, A_WIDTH, tm), lambda b, i: (b, 0, i)),
                   pl.BlockSpec((1, tm, A_KV_WIDTH), lambda b, i: (b, i, 0)),
                   pl.BlockSpec((1, A_KV_WIDTH, tm), lambda b, i: (b, 0, i)),
                   pl.BlockSpec((1, A_WIDTH, tm), lambda b, i: (b, 0, i))],
        out_shape=[jax.ShapeDtypeStruct((bsz, A_WIDTH, n), bf16),
                   jax.ShapeDtypeStruct((bsz, n, A_KV_WIDTH), bf16),
                   jax.ShapeDtypeStruct((bsz, A_KV_WIDTH, n), bf16),
                   jax.ShapeDtypeStruct((bsz, A_WIDTH, n), bf16)],
        compiler_params=_params("parallel", "parallel"),
        name="in_proj_c",
    )(x, mod, gpre, wt, wk, cost, sint, cos, sin)


def _attn_kernel(sink_ref, qt_ref, k_ref, vt_ref, kc_ref, vct_ref, zt_ref, ogt_ref, s_scr, *, banded):
    nc = kc_ref.shape[1]
    if banded:
        n = k_ref.shape[1]
        start = pl.program_id(1) * A_BLOCK
        bs = pl.multiple_of(jnp.clip(start - A_WINDOW, 0, n - A_BAND), A_BLOCK)
        kj = bs + lax.broadcasted_iota(jnp.int32, (A_BAND, A_BLOCK), 0)
        qi = start + lax.broadcasted_iota(jnp.int32, (A_BAND, A_BLOCK), 1)
        bias = jnp.where(jnp.abs(qi - kj) <= A_WINDOW, 0.0, -jnp.inf)
        bias = jnp.concatenate([bias] * A_GROUP, axis=1)
    for g in range(A_KV_HEADS):
        heads = range(g * A_GROUP, (g + 1) * A_GROUP)
        pair = slice(128 * (g // 2), 128 * (g // 2) + 128)
        kvrows = slice(g * A_HEAD_DIM, (g + 1) * A_HEAD_DIM)
        qg = jnp.concatenate([qt_ref[0, h * A_HEAD_DIM:(h + 1) * A_HEAD_DIM, :] for h in heads], axis=1)
        zero = jnp.zeros_like(qg)
        qpad = jnp.concatenate([qg, zero] if g % 2 == 0 else [zero, qg], axis=0)
        s_scr[g, 0:nc, :] = jnp.dot(kc_ref[0, :, pair], qpad, preferred_element_type=f32)
        if banded:
            s_scr[g, nc:, :] = jnp.dot(k_ref[0, pl.ds(bs, A_BAND), pair], qpad,
                                       preferred_element_type=f32) + bias
        sk = jnp.concatenate([jnp.full((1, A_BLOCK), sink_ref[h] * LOG2E, f32) for h in heads], axis=1)
        s = s_scr[g]
        m = jnp.maximum(jnp.max(s, axis=0, keepdims=True), sk)
        p = jnp.exp2(s - m)
        inv = 1.0 / (jnp.sum(p, axis=0, keepdims=True) + jnp.exp2(sk - m))
        pb = p.astype(bf16)
        ot = jnp.dot(vct_ref[0, kvrows, :], pb[0:nc], preferred_element_type=f32)
        if banded:
            ot = ot + jnp.dot(vt_ref[0, kvrows, pl.ds(bs, A_BAND)], pb[nc:], preferred_element_type=f32)
        ot = ot * inv
        for a, h in enumerate(heads):
            rows = slice(h * A_HEAD_DIM, (h + 1) * A_HEAD_DIM)
            gate = _silu(zt_ref[0, rows, :].astype(f32))
            ogt_ref[0, rows, :] = (ot[:, a * A_BLOCK:(a + 1) * A_BLOCK] * gate).astype(bf16)


def _attn(sink, qt, k, vt, kc, vct, zt, banded):
    bsz, _, nq = qt.shape
    nk, nc = k.shape[1], kc.shape[1]
    nkeys = nc + (A_BAND if banded else 0)
    return pl.pallas_call(
        functools.partial(_attn_kernel, banded=banded),
        grid=(bsz, nq // A_BLOCK),
        in_specs=[pl.BlockSpec(memory_space=pltpu.SMEM),
                  pl.BlockSpec((1, A_WIDTH, A_BLOCK), lambda b, i: (b, 0, i)),
                  pl.BlockSpec((1, nk, A_KV_WIDTH), lambda b, i: (b, 0, 0)),
                  pl.BlockSpec((1, A_KV_WIDTH, nk), lambda b, i: (b, 0, 0)),
                  pl.BlockSpec((1, nc, A_KV_WIDTH), lambda b, i: (b, 0, 0)),
                  pl.BlockSpec((1, A_KV_WIDTH, nc), lambda b, i: (b, 0, 0)),
                  pl.BlockSpec((1, A_WIDTH, A_BLOCK), lambda b, i: (b, 0, i))],
        out_specs=pl.BlockSpec((1, A_WIDTH, A_BLOCK), lambda b, i: (b, 0, i)),
        out_shape=jax.ShapeDtypeStruct((bsz, A_WIDTH, nq), bf16),
        scratch_shapes=[pltpu.VMEM((A_KV_HEADS, nkeys, A_GROUP * A_BLOCK), f32)],
        compiler_params=_params("parallel", "parallel"),
        name="attention_banded" if banded else "attention_context",
    )(sink, qt, k, vt, kc, vct, zt)


def _out_c_kernel(at_ref, x_ref, mod_ref, gpost_ref, wo_ref, o_ref):
    y = jnp.dot(at_ref[0].T, wo_ref[...], preferred_element_type=f32)
    o_ref[0] = _gated_residual(x_ref[0], y, gpost_ref[...], mod_ref[0])


def _out_c(at, x, mod, gpost, wo, tm):
    bsz, n, _ = x.shape
    return pl.pallas_call(
        _out_c_kernel,
        grid=(bsz, n // tm),
        in_specs=[pl.BlockSpec((1, A_WIDTH, tm), lambda b, i: (b, 0, i)),
                  pl.BlockSpec((1, tm, D_MODEL), lambda b, i: (b, i, 0)),
                  pl.BlockSpec((1, 1, 3 * D_MODEL), lambda b, i: (b % mod.shape[0], 0, 0)),
                  pl.BlockSpec((1, D_MODEL), lambda b, i: (0, 0)),
                  pl.BlockSpec((A_WIDTH, D_MODEL), lambda b, i: (0, 0))],
        out_specs=pl.BlockSpec((1, tm, D_MODEL), lambda b, i: (b, i, 0)),
        out_shape=jax.ShapeDtypeStruct((bsz, n, D_MODEL), f32),
        compiler_params=_params("parallel", "parallel"),
        name="out_proj_c",
    )(at, x, mod, gpost, wo)


def _rope_tables(n):
    rows = n // GRID_W
    row = jnp.repeat(jnp.arange(rows), GRID_W).astype(f32)
    col = jnp.tile(jnp.arange(GRID_W), rows).astype(f32)
    n_freq = A_HEAD_DIM // 4
    inv = ROPE_BASE ** (-jnp.arange(n_freq, dtype=f32) / n_freq)
    ang = jnp.concatenate([row[:, None] * inv, col[:, None] * inv], -1)
    cos, sin = jnp.cos(ang), jnp.sin(ang)
    cos_t = jnp.tile(jnp.concatenate([cos, cos], -1), (1, 128 // A_HEAD_DIM))
    sin_t = jnp.tile(jnp.concatenate([-sin, sin], -1), (1, 128 // A_HEAD_DIM))
    return cos.T, sin.T, cos_t, sin_t


def _split_pairs_columns(w, heads):
    perm = np.concatenate([np.arange(0, A_HEAD_DIM, 2), np.arange(1, A_HEAD_DIM, 2)])
    idx = (np.arange(heads)[:, None] * A_HEAD_DIM + perm[None, :]).reshape(-1)
    return w[:, idx]


def kernel(x, c, ctx, c_ctx, w_mod, b_mod, g_pre, g_post, ab_w_in, ab_b_gate, ab_conv, ab_mnorm,
           ab_pool_w, ab_pool_scale, ab_w_out, c_w_in, c_sink, c_w_out):
    bsz, n, _ = x.shape
    n_ctx = ctx.shape[1]
    assert bsz < 16
    cs = jnp.zeros((16, D_MODEL), f32).at[:bsz].set(c).at[bsz].set(c_ctx)
    mod_all = _modulation(cs, w_mod, b_mod)
    cost, sint, cos, sin = _rope_tables(n)
    tm_x = 1024

    for l in range(DEPTH):
        j = l // 2
        last = l == DEPTH - 1
        mod_x = mod_all[l, :bsz].reshape(bsz, 1, 3 * D_MODEL)
        mod_c = mod_all[l, bsz:bsz + 1].reshape(1, 1, 3 * D_MODEL)
        gpre = g_pre[l].reshape(1, D_MODEL)
        gpost = g_post[l].reshape(1, D_MODEL)
        if l % 2 == 0:
            w = ab_w_in[j].astype(bf16)
            w_tok = jnp.concatenate([w[:, :2 * M_WIDTH], w[:, 5 * M_WIDTH:GATE_OFF]], axis=1)
            wt_ch = w[:, 2 * M_WIDTH:5 * M_WIDTH].T
            wg = jnp.pad(w[:, GATE_OFF:], ((0, 0), (0, GATE_PAD - N_GATE)))
            bg = jnp.pad(ab_b_gate[j], (0, GATE_PAD - N_GATE)).reshape(1, GATE_PAD)
            k_scale = jnp.concatenate([jnp.ones((M_WIDTH,), f32), jnp.full((M_WIDTH,), M_HEAD_DIM ** -0.5, f32)])
            cw = ab_conv[j] * k_scale
            mnorm_tab = jnp.broadcast_to(ab_mnorm[j][:, None], (M_WIDTH, 128))

            def project(a, mod_a, tm):
                q, k, xp, zp, g = _in_tok(a, mod_a, gpre, w_tok, wg, bg, cw, tm)
                vt, ot, zt = _in_ch(a, mod_a, gpre, wt_ch, tm)
                return (q, k, vt, ot, zt, g), xp, zp

            lat, xpx, zpx = project(x, mod_x, 512)
            con, xpc, zpc = project(ctx, mod_c, n_ctx)
            ymx, ymc = _mlstm(lat, con, mnorm_tab)
            wo = ab_w_out[j].astype(bf16)
            pw = ab_pool_w[j].astype(bf16)
            ps = ab_pool_scale[j].reshape(1, P_WIDTH)
            x = _out_ab(ymx, xpx, zpx, x, mod_x, gpost, wo, pw, ps, 512)
            if not last:
                ctx = _out_ab(ymc, xpc, zpc, ctx, mod_c, gpost, wo, pw, ps, n_ctx)
        else:
            w = c_w_in[j].astype(bf16)
            wt = jnp.concatenate([_split_pairs_columns(w[:, :A_WIDTH], A_HEADS),
                                  w[:, A_WIDTH + A_KV_WIDTH:]], axis=1).T
            wk = _split_pairs_columns(w[:, A_WIDTH:A_WIDTH + A_KV_WIDTH], A_KV_HEADS)
            qx, kx, vx, zx = _in_c(x, mod_x, gpre, wt, wk, cost, sint, cos, sin, 512, True)
            qc, kc, vc, zc = _in_c(ctx, mod_c, gpre, wt, wk, cost[:, :n_ctx], sint[:, :n_ctx],
                                   cos[:n_ctx], sin[:n_ctx], n_ctx, False)
            wo = c_w_out[j].astype(bf16)
            ax = _attn(c_sink[j], qx, kx, vx, kc, vc, zx, True)
            x = _out_c(ax, x, mod_x, gpost, wo, tm_x)
            if not last:
                ac = _attn(c_sink[j], qc, kc, vc, kc, vc, zc, False)
                ctx = _out_c(ac, ctx, mod_c, gpost, wo, n_ctx)
    return x
```

```python
import functools

import jax
import jax.numpy as jnp
import numpy as np
from jax import lax
from jax.experimental import pallas as pl
from jax.experimental.pallas import tpu as pltpu

f32 = jnp.float32
bf16 = jnp.bfloat16

D_MODEL = 1024
DEPTH = 4
EPS = 1e-6
GRID_W = 64

M_HEADS = 4
M_HEAD_DIM = 256
M_WIDTH = 1024
M_CHUNK = 128
P_WINDOWS = (2, 4, 8, 16)
P_GROUP_DIM = 256
P_WIDTH = 1024
N_GATE = 16
GATE_OFF = 5 * M_WIDTH + 2 * P_WIDTH
GATE_PAD = 128

A_HEADS = 16
A_KV_HEADS = 4
A_GROUP = 4
A_HEAD_DIM = 64
A_WIDTH = 1024
A_KV_WIDTH = 256
A_WINDOW = 128
A_BLOCK = 128
A_BAND = A_BLOCK + 2 * A_WINDOW
ROPE_BASE = 10000.0

BF16_ROWS = 16
VMEM_LIMIT = 52 * 1024 * 1024


def _params(*sem):
    return pltpu.CompilerParams(dimension_semantics=sem, vmem_limit_bytes=VMEM_LIMIT)


def _silu(a):
    return a * jax.nn.sigmoid(a)


def _modulated_norm(x, gpre, mod):
    xn = x * lax.rsqrt(jnp.mean(x * x, -1, keepdims=True) + EPS) * gpre
    return xn * (1.0 + mod[:, D_MODEL:2 * D_MODEL]) + mod[:, :D_MODEL]


def _gated_residual(x, y, gpost, mod):
    r = y * lax.rsqrt(jnp.mean(y * y, -1, keepdims=True) + EPS) * gpost
    return x + mod[:, 2 * D_MODEL:] * r


def _mod_kernel(cs_ref, w_ref, b_ref, o_ref):
    a = _silu(cs_ref[...])
    o_ref[0] = jnp.dot(a, w_ref[0], preferred_element_type=f32,
                       precision=lax.Precision.HIGHEST) + b_ref[0]


def _modulation(cs, w_mod, b_mod):
    tn = 512
    return pl.pallas_call(
        _mod_kernel,
        grid=(DEPTH, 3 * D_MODEL // tn),
        in_specs=[pl.BlockSpec((16, D_MODEL), lambda l, j: (0, 0)),
                  pl.BlockSpec((1, D_MODEL, tn), lambda l, j: (l, 0, j)),
                  pl.BlockSpec((1, 1, tn), lambda l, j: (l, 0, j))],
        out_specs=pl.BlockSpec((1, 16, tn), lambda l, j: (l, 0, j)),
        out_shape=jax.ShapeDtypeStruct((DEPTH, 16, 3 * D_MODEL), f32),
        compiler_params=_params("parallel", "parallel"),
        name="modulation",
    )(cs, w_mod, b_mod.reshape(DEPTH, 1, 3 * D_MODEL))


NT_DIMS = (((1,), (1,)), ((), ()))
C_ROWS = 512
HALO = 16


def _in_tok_kernel(x_ref, xprev_ref, xnext_ref, mod_ref, gpre_ref, w_ref, wg_ref, bg_ref, cw_ref,
                   q_ref, k_ref, xp_ref, zp_ref, g_ref, h_scr, r_scr):
    i, nt, j = pl.program_id(1), pl.num_programs(1), pl.program_id(2)
    tm = x_ref.shape[1]

    @pl.when(j == 0)
    def _():
        def normed(a):
            return _modulated_norm(a, gpre_ref[...], mod_ref[0]).astype(bf16)

        h_scr[0:HALO, :] = normed(xprev_ref[0])
        h_scr[HALO:HALO + tm, :] = normed(x_ref[0])
        h_scr[HALO + tm:, :] = normed(xnext_ref[0])
        g_ref[0] = jnp.dot(h_scr[HALO:HALO + tm, :], wg_ref[...], preferred_element_type=f32) + bg_ref[...]
        r_scr[...] = jnp.dot(h_scr[...], w_ref[...], preferred_element_type=f32)

        @pl.when(i == 0)
        def _():
            r_scr[0:HALO, :] = jnp.zeros((HALO, r_scr.shape[1]), f32)

        @pl.when(i == nt - 1)
        def _():
            r_scr[HALO + tm:, :] = jnp.zeros((HALO, r_scr.shape[1]), f32)

        for half, out_ref in enumerate((q_ref, k_ref)):
            cols = slice(half * M_WIDTH, (half + 1) * M_WIDTH)
            out_ref[0] = (cw_ref[0:1, cols] * r_scr[HALO - 1:HALO - 1 + tm, cols]
                          + cw_ref[1:2, cols] * r_scr[HALO:HALO + tm, cols]
                          + cw_ref[2:3, cols] * r_scr[HALO + 1:HALO + 1 + tm, cols]).astype(bf16)

    @pl.when(j == 1)
    def _():
        r = jnp.dot(h_scr[HALO:HALO + tm, :], w_ref[...], preferred_element_type=f32)
        xp_ref[0] = r[:, :P_WIDTH].astype(bf16)
        zp_ref[0] = r[:, P_WIDTH:].astype(bf16)


def _in_tok(x, mod, gpre, w, wg, bg, cw, tm):
    bsz, n, _ = x.shape
    th, nh = tm // HALO, n // HALO
    tok = jax.ShapeDtypeStruct((bsz, n, M_WIDTH), bf16)
    tok_spec = pl.BlockSpec((1, tm, M_WIDTH), lambda b, i, j: (b, i, 0))
    return pl.pallas_call(
        _in_tok_kernel,
        grid=(bsz, n // tm, 2),
        in_specs=[pl.BlockSpec((1, tm, D_MODEL), lambda b, i, j: (b, i, 0)),
                  pl.BlockSpec((1, HALO, D_MODEL), lambda b, i, j: (b, jnp.maximum(i * th - 1, 0), 0)),
                  pl.BlockSpec((1, HALO, D_MODEL), lambda b, i, j: (b, jnp.minimum((i + 1) * th, nh - 1), 0)),
                  pl.BlockSpec((1, 1, 3 * D_MODEL), lambda b, i, j: (b % mod.shape[0], 0, 0)),
                  pl.BlockSpec((1, D_MODEL), lambda b, i, j: (0, 0)),
                  pl.BlockSpec((D_MODEL, 2 * M_WIDTH), lambda b, i, j: (0, j)),
                  pl.BlockSpec((D_MODEL, GATE_PAD), lambda b, i, j: (0, 0)),
                  pl.BlockSpec((1, GATE_PAD), lambda b, i, j: (0, 0)),
                  pl.BlockSpec((3, 2 * M_WIDTH), lambda b, i, j: (0, 0))],
        out_specs=[tok_spec, tok_spec, tok_spec, tok_spec,
                   pl.BlockSpec((1, tm, GATE_PAD), lambda b, i, j: (b, i, 0))],
        out_shape=[tok, tok, tok, tok, jax.ShapeDtypeStruct((bsz, n, GATE_PAD), f32)],
        scratch_shapes=[pltpu.VMEM((tm + 2 * HALO, D_MODEL), bf16),
                        pltpu.VMEM((tm + 2 * HALO, 2 * M_WIDTH), f32)],
        compiler_params=_params("parallel", "parallel", "arbitrary"),
        name="in_proj_ab_tok",
    )(x, x, x, mod, gpre, w, wg, bg, cw)


def _in_ch_kernel(x_ref, mod_ref, gpre_ref, wt_ref, vt_ref, ot_ref, zt_ref):
    hb = _modulated_norm(x_ref[0], gpre_ref[...], mod_ref[0]).astype(bf16)
    for idx, out_ref in enumerate((vt_ref, ot_ref, zt_ref)):
        for lo in range(0, M_WIDTH, C_ROWS):
            rows = wt_ref[idx * M_WIDTH + lo:idx * M_WIDTH + lo + C_ROWS, :]
            out_ref[0, lo:lo + C_ROWS, :] = lax.dot_general(
                rows, hb, NT_DIMS, preferred_element_type=f32).astype(bf16)


def _in_ch(x, mod, gpre, wt, tm):
    bsz, n, _ = x.shape
    ch = jax.ShapeDtypeStruct((bsz, M_WIDTH, n), bf16)
    ch_spec = pl.BlockSpec((1, M_WIDTH, tm), lambda b, i: (b, 0, i))
    return pl.pallas_call(
        _in_ch_kernel,
        grid=(bsz, n // tm),
        in_specs=[pl.BlockSpec((1, tm, D_MODEL), lambda b, i: (b, i, 0)),
                  pl.BlockSpec((1, 1, 3 * D_MODEL), lambda b, i: (b % mod.shape[0], 0, 0)),
                  pl.BlockSpec((1, D_MODEL), lambda b, i: (0, 0)),
                  pl.BlockSpec(wt.shape, lambda b, i: (0, 0))],
        out_specs=[ch_spec, ch_spec, ch_spec],
        out_shape=[ch, ch, ch],
        compiler_params=_params("parallel", "parallel"),
        name="in_proj_ab_ch",
    )(x, mod, gpre, wt)


def _log_sigmoid(a):
    return jnp.minimum(a, 0.0) - jnp.log1p(jnp.exp(-jnp.abs(a)))


def _gate_rows(g_ref, r_scr):
    g = g_ref[0]
    n = g.shape[0]
    sub = lax.broadcasted_iota(jnp.int32, (N_GATE, GATE_PAD), 0)
    lane = lax.broadcasted_iota(jnp.int32, (N_GATE, GATE_PAD), 1)
    pick = jnp.where(lane == (sub % 4) * M_HEADS + sub // 4, 1.0, 0.0).astype(bf16)
    rows = jnp.zeros((N_GATE, n), f32)
    rest = g
    for _ in range(3):
        piece = rest.astype(bf16)
        rest = rest - piece.astype(f32)
        rows = rows + lax.dot_general(pick, piece, NT_DIMS, preferred_element_type=f32)
    logf = _log_sigmoid(rows)
    pos = lax.broadcasted_iota(jnp.int32, (N_GATE, n), 1) % M_CHUNK
    pre, suf = logf, logf
    shift = 1
    while shift < M_CHUNK:
        pre = pre + jnp.where(pos >= shift, pltpu.roll(pre, shift, 1), 0.0)
        suf = suf + jnp.where(pos < M_CHUNK - shift, pltpu.roll(suf, n - shift, 1), 0.0)
        shift *= 2
    kind = lax.broadcasted_iota(jnp.int32, (N_GATE, n), 0) % 4
    final = jnp.where(kind == 1, pre, jnp.where(kind == 3, suf, rows))
    for h in range(M_HEADS):
        r_scr[h, 0:4, :] = final[4 * h:4 * h + 4]


STATE_ROWS = M_HEAD_DIM + BF16_ROWS


def _mlstm_chunk(seg, consts, state, start, direction, first):
    q_ref, k_ref, vt_ref, ot_ref, zt_ref, r_scr, acc, yt_ref = seg
    mn_ref, head = consts
    ct_scr, m_scr = state
    if not isinstance(start, int):
        start = pl.multiple_of(start, M_CHUNK)
    span = pl.ds(start, M_CHUNK)
    dk = M_HEAD_DIM

    q = q_ref[0, span, :]
    k = k_ref[0, span, :]
    vt = vt_ref[0, :, span]

    li = r_scr[head, 2 * direction:2 * direction + 1, span]
    bcum = r_scr[head, 2 * direction + 1:2 * direction + 2, span]
    si = lax.broadcasted_iota(jnp.int32, (M_CHUNK, M_CHUNK), 0)
    ti = lax.broadcasted_iota(jnp.int32, (M_CHUNK, M_CHUNK), 1)
    causal = (si <= ti) if direction == 0 else (si >= ti)
    b_last = bcum[:, M_CHUNK - 1:M_CHUNK] if direction == 0 else bcum[:, 0:1]

    m_prev = m_scr[direction]
    ct = ct_scr[direction]
    n_mem = ct[dk:dk + 1, :]

    key_side = jnp.transpose(jnp.broadcast_to(li - bcum, (M_CHUNK, M_CHUNK)))
    dmat = jnp.where(causal, bcum + key_side, -jnp.inf)
    inter = bcum + m_prev
    m_t = jnp.maximum(inter, jnp.max(dmat, axis=0, keepdims=True))
    w_inter = jnp.exp(inter - m_t)
    s = lax.dot_general(k, q, NT_DIMS, preferred_element_type=f32) * jnp.exp(dmat - m_t)

    tail = lax.broadcasted_iota(jnp.int32, (BF16_ROWS, dk), 0)
    n_hi = n_mem.astype(bf16).astype(f32)
    extra = jnp.where(tail == 0, n_hi, jnp.where(tail == 1, n_mem - n_hi, 0.0))
    lhs = jnp.concatenate([ct[:dk].astype(bf16), extra.astype(bf16)], axis=0)
    from_state = lax.dot_general(lhs, q, NT_DIMS, preferred_element_type=f32)
    num = w_inter * from_state[:dk] + jnp.dot(vt, s.astype(bf16), preferred_element_type=f32)
    den = (w_inter * (from_state[dk:dk + 1] + from_state[dk + 1:dk + 2])
           + jnp.sum(s, axis=0, keepdims=True))
    h = num * (1.0 / jnp.maximum(jnp.abs(den), jnp.exp(-m_t)))

    d_last = b_last - bcum + li
    m_new = jnp.maximum(b_last + m_prev, jnp.max(d_last, axis=-1, keepdims=True))
    w_row = jnp.exp(d_last - m_new)
    decay = jnp.exp(b_last + m_prev - m_new)
    w_hi = w_row.astype(bf16).astype(f32)
    w_extra = jnp.where(tail[:, :M_CHUNK] == 0, w_hi, jnp.where(tail[:, :M_CHUNK] == 1, w_row - w_hi, 0.0))
    va = jnp.concatenate([(vt.astype(f32) * w_row).astype(bf16), w_extra.astype(bf16)], axis=0)
    update = jnp.dot(va, k, preferred_element_type=f32)
    ct_scr[direction, 0:dk, :] = decay * ct[:dk] + update[:dk]
    n_new = decay * n_mem + update[dk:dk + 1] + update[dk + 1:dk + 2]
    ct_scr[direction, dk:, :] = jnp.where(tail == 0, n_new, 0.0)
    m_scr[direction] = m_new

    if first:
        acc[:, span] = h
    else:
        hg = (acc[:, span] + h) * jax.nn.sigmoid(ot_ref[0, :, span].astype(f32))
        hn = hg * lax.rsqrt(jnp.mean(hg * hg, axis=0, keepdims=True) + EPS) * mn_ref[...]
        yt_ref[0, :, span] = (hn * _silu(zt_ref[0, :, span].astype(f32))).astype(bf16)


def _mlstm_kernel(qx_ref, kx_ref, vx_ref, ox_ref, zx_ref, gx_ref,
                  qc_ref, kc_ref, vc_ref, oc_ref, zc_ref, gc_ref,
                  mn_ref, yx_ref, yc_ref,
                  rx_scr, rc_scr, accx, accc, ct_scr, m_scr):
    head = pl.program_id(1)
    nx, nc = qx_ref.shape[1], qc_ref.shape[1]

    @pl.when(head == 0)
    def _():
        _gate_rows(gx_ref, rx_scr)
        _gate_rows(gc_ref, rc_scr)

    ct_scr[...] = jnp.zeros_like(ct_scr)
    m_scr[...] = jnp.zeros_like(m_scr)

    consts = (mn_ref, head)
    state = (ct_scr, m_scr)
    ctx = (qc_ref, kc_ref, vc_ref, oc_ref, zc_ref, rc_scr, accc, yc_ref)
    lat = (qx_ref, kx_ref, vx_ref, ox_ref, zx_ref, rx_scr, accx, yx_ref)
    step = functools.partial(_mlstm_chunk, consts=consts, state=state)

    ncc = nc // M_CHUNK
    assert ncc == 2
    step(ctx, start=0, direction=0, first=True)
    step(ctx, start=M_CHUNK, direction=1, first=True)
    step(ctx, start=M_CHUNK, direction=0, first=False)
    step(ctx, start=0, direction=1, first=False)

    nxc = nx // M_CHUNK

    def meet(first):
        def body(i, carry):
            step(lat, start=i * M_CHUNK, direction=0, first=first)
            step(lat, start=(nxc - 1 - i) * M_CHUNK, direction=1, first=first)
            return carry
        return body

    lax.fori_loop(0, nxc // 2, meet(True), 0, unroll=8)
    lax.fori_loop(nxc // 2, nxc, meet(False), 0, unroll=4)


def _mlstm(lat, ctx, mnorm_tab):
    bsz, nx, _ = lat[0].shape
    nc = ctx[0].shape[1]
    hd = M_HEAD_DIM

    def specs(n):
        tok = pl.BlockSpec((1, n, hd), lambda b, h: (b, 0, h))
        ch = pl.BlockSpec((1, hd, n), lambda b, h: (b, h, 0))
        return [tok, tok, ch, ch, ch, pl.BlockSpec((1, n, GATE_PAD), lambda b, h: (b, 0, 0))]

    return pl.pallas_call(
        _mlstm_kernel,
        grid=(bsz, M_HEADS),
        in_specs=specs(nx) + specs(nc) + [pl.BlockSpec((hd, 128), lambda b, h: (h, 0))],
        out_specs=[pl.BlockSpec((1, hd, nx), lambda b, h: (b, h, 0)),
                   pl.BlockSpec((1, hd, nc), lambda b, h: (b, h, 0))],
        out_shape=[jax.ShapeDtypeStruct((bsz, M_WIDTH, nx), bf16),
                   jax.ShapeDtypeStruct((bsz, M_WIDTH, nc), bf16)],
        scratch_shapes=[pltpu.VMEM((M_HEADS, 8, nx), f32), pltpu.VMEM((M_HEADS, 8, nc), f32),
                        pltpu.VMEM((hd, nx), f32), pltpu.VMEM((hd, nc), f32),
                        pltpu.VMEM((2, STATE_ROWS, hd), f32), pltpu.VMEM((2, 1, 1), f32)],
        compiler_params=_params("parallel", "arbitrary"),
        name="mlstm",
    )(*lat, *ctx, mnorm_tab)


P_HALO = 16


def _out_ab_kernel(ym_ref, xp_ref, xpp_ref, xpn_ref, zp_ref, x_ref, mod_ref, gpost_ref,
                   wo_ref, pw_ref, ps_ref, o_ref, ext_scr):
    i, nt = pl.program_id(1), pl.num_programs(1)
    tm = x_ref.shape[1]
    ext_scr[0:P_HALO, :] = jnp.where(i > 0, xpp_ref[0].astype(f32), 0.0)
    ext_scr[P_HALO:P_HALO + tm, :] = xp_ref[0].astype(f32)
    ext_scr[P_HALO + tm:, :] = jnp.where(i < nt - 1, xpn_ref[0].astype(f32), 0.0)

    t = i * tm + lax.broadcasted_iota(jnp.int32, (tm, 1), 0)
    n = nt * tm
    y = jnp.dot(ym_ref[0].T, wo_ref[0:M_WIDTH, :], preferred_element_type=f32)
    ext_rows = tm + 2 * P_HALO
    for g, win in enumerate(P_WINDOWS):
        half = win // 2
        cols = slice(g * P_GROUP_DIM, (g + 1) * P_GROUP_DIM)
        e = ext_scr[:, cols]
        run, width = e, 1
        while width < win:
            run = run + pltpu.roll(run, ext_rows - width, 0)
            width *= 2
        total = pltpu.roll(run, half, 0)[P_HALO:P_HALO + tm]
        count = (jnp.minimum(t + half, n) - jnp.maximum(t - half, 0)).astype(f32)
        pooled = total * (1.0 / count) - e[P_HALO:P_HALO + tm]
        mixed = jnp.dot(pooled.astype(bf16), pw_ref[g], preferred_element_type=f32) * ps_ref[:, cols]
        yp = (mixed * _silu(zp_ref[0, :, cols].astype(f32))).astype(bf16)
        y = y + jnp.dot(yp, wo_ref[M_WIDTH + g * P_GROUP_DIM:M_WIDTH + (g + 1) * P_GROUP_DIM, :],
                        preferred_element_type=f32)
    o_ref[0] = _gated_residual(x_ref[0], y, gpost_ref[...], mod_ref[0])


def _out_ab(ymt, xp, zp, x, mod, gpost, wo, pw, ps, tm):
    bsz, n, _ = x.shape
    nh = n // P_HALO
    th = tm // P_HALO
    return pl.pallas_call(
        _out_ab_kernel,
        grid=(bsz, n // tm),
        in_specs=[pl.BlockSpec((1, M_WIDTH, tm), lambda b, i: (b, 0, i)),
                  pl.BlockSpec((1, tm, P_WIDTH), lambda b, i: (b, i, 0)),
                  pl.BlockSpec((1, P_HALO, P_WIDTH), lambda b, i: (b, jnp.maximum(i * th - 1, 0), 0)),
                  pl.BlockSpec((1, P_HALO, P_WIDTH), lambda b, i: (b, jnp.minimum((i + 1) * th, nh - 1), 0)),
                  pl.BlockSpec((1, tm, P_WIDTH), lambda b, i: (b, i, 0)),
                  pl.BlockSpec((1, tm, D_MODEL), lambda b, i: (b, i, 0)),
                  pl.BlockSpec((1, 1, 3 * D_MODEL), lambda b, i: (b % mod.shape[0], 0, 0)),
                  pl.BlockSpec((1, D_MODEL), lambda b, i: (0, 0)),
                  pl.BlockSpec((M_WIDTH + P_WIDTH, D_MODEL), lambda b, i: (0, 0)),
                  pl.BlockSpec((len(P_WINDOWS), P_GROUP_DIM, P_GROUP_DIM), lambda b, i: (0, 0, 0)),
                  pl.BlockSpec((1, P_WIDTH), lambda b, i: (0, 0))],
        out_specs=pl.BlockSpec((1, tm, D_MODEL), lambda b, i: (b, i, 0)),
        out_shape=jax.ShapeDtypeStruct((bsz, n, D_MODEL), f32),
        scratch_shapes=[pltpu.VMEM((tm + 2 * P_HALO, P_WIDTH), f32)],
        compiler_params=_params("parallel", "parallel"),
        name="out_proj_ab",
    )(ymt, xp, xp, xp, zp, x, mod, gpost, wo, pw, ps)


def _rope(a, cos, sin):
    lane = lax.broadcasted_iota(jnp.int32, a.shape, 1)
    half = A_HEAD_DIM // 2
    partner = jnp.where(lane % A_HEAD_DIM < half,
                        pltpu.roll(a, a.shape[1] - half, 1), pltpu.roll(a, half, 1))
    return a * cos + partner * sin


LOG2E = 1.4426950408889634


def _in_c_kernel(x_ref, mod_ref, gpre_ref, wt_ref, wk_ref, cost_ref, sint_ref, cos_ref, sin_ref,
                 qt_ref, k_ref, vt_ref, zt_ref, *, rope):
    hb = _modulated_norm(x_ref[0], gpre_ref[...], mod_ref[0]).astype(bf16)

    def channel_major(lo, rows):
        return lax.dot_general(wt_ref[lo:lo + rows, :], hb, NT_DIMS, preferred_element_type=f32)

    half = A_HEAD_DIM // 2
    q_scale = (A_HEAD_DIM ** -0.5) * LOG2E
    for lo in range(0, A_WIDTH, C_ROWS):
        a = channel_major(lo, C_ROWS)
        for h in range(C_ROWS // A_HEAD_DIM):
            x1 = a[h * A_HEAD_DIM:h * A_HEAD_DIM + half]
            x2 = a[h * A_HEAD_DIM + half:(h + 1) * A_HEAD_DIM]
            if rope:
                c, s = cost_ref[...], sint_ref[...]
                x1, x2 = x1 * c - x2 * s, x1 * s + x2 * c
            r0 = lo + h * A_HEAD_DIM
            qt_ref[0, r0:r0 + half, :] = (x1 * q_scale).astype(bf16)
            qt_ref[0, r0 + half:r0 + A_HEAD_DIM, :] = (x2 * q_scale).astype(bf16)
    vt_ref[0] = channel_major(A_WIDTH, A_KV_WIDTH).astype(bf16)
    for lo in range(0, A_WIDTH, C_ROWS):
        zt_ref[0, lo:lo + C_ROWS, :] = channel_major(A_WIDTH + A_KV_WIDTH + lo, C_ROWS).astype(bf16)
    for j in range(A_KV_WIDTH // 128):
        kk = jnp.dot(hb, wk_ref[:, j * 128:(j + 1) * 128], preferred_element_type=f32)
        if rope:
            kk = _rope(kk, cos_ref[...], sin_ref[...])
        k_ref[0, :, j * 128:(j + 1) * 128] = kk.astype(bf16)


def _in_c(x, mod, gpre, wt, wk, cost, sint, cos, sin, tm, rope):
    bsz, n, _ = x.shape
    half = A_HEAD_DIM // 2
    return pl.pallas_call(
        functools.partial(_in_c_kernel, rope=rope),
        grid=(bsz, n // tm),
        in_specs=[pl.BlockSpec((1, tm, D_MODEL), lambda b, i: (b, i, 0)),
                  pl.BlockSpec((1, 1, 3 * D_MODEL), lambda b, i: (b % mod.shape[0], 0, 0)),
                  pl.BlockSpec((1, D_MODEL), lambda b, i: (0, 0)),
                  pl.BlockSpec(wt.shape, lambda b, i: (0, 0)),
                  pl.BlockSpec(wk.shape, lambda b, i: (0, 0)),
                  pl.BlockSpec((half, tm), lambda b, i: (0, i)),
                  pl.BlockSpec((half, tm), lambda b, i: (0, i)),
                  pl.BlockSpec((tm, 128), lambda b, i: (i, 0)),
                  pl.BlockSpec((tm, 128), lambda b, i: (i, 0))],
        out_specs=[pl.BlockSpec((1, A_WIDTH, tm), lambda b, i: (b, 0, i)),
                   pl.BlockSpec((1, tm, A_KV_WIDTH), lambda b, i: (b, i, 0)),
                   pl.BlockSpec((1, A_KV_WIDTH, tm), lambda b, i: (b, 0, i)),
                   pl.BlockSpec((1, A_WIDTH, tm), lambda b, i: (b, 0, i))],
        out_shape=[jax.ShapeDtypeStruct((bsz, A_WIDTH, n), bf16),
                   jax.ShapeDtypeStruct((bsz, n, A_KV_WIDTH), bf16),
                   jax.ShapeDtypeStruct((bsz, A_KV_WIDTH, n), bf16),
                   jax.ShapeDtypeStruct((bsz, A_WIDTH, n), bf16)],
        compiler_params=_params("parallel", "parallel"),
        name="in_proj_c",
    )(x, mod, gpre, wt, wk, cost, sint, cos, sin)


def _attn_kernel(sink_ref, qt_ref, k_ref, vt_ref, kc_ref, vct_ref, zt_ref, ogt_ref, s_scr, *, banded):
    nc = kc_ref.shape[1]
    if banded:
        n = k_ref.shape[1]
        start = pl.program_id(1) * A_BLOCK
        bs = pl.multiple_of(jnp.clip(start - A_WINDOW, 0, n - A_BAND), A_BLOCK)
        kj = bs + lax.broadcasted_iota(jnp.int32, (A_BAND, A_BLOCK), 0)
        qi = start + lax.broadcasted_iota(jnp.int32, (A_BAND, A_BLOCK), 1)
        bias = jnp.where(jnp.abs(qi - kj) <= A_WINDOW, 0.0, -jnp.inf)
        bias = jnp.concatenate([bias] * A_GROUP, axis=1)
    def scores(g):
        heads = range(g * A_GROUP, (g + 1) * A_GROUP)
        pair = slice(128 * (g // 2), 128 * (g // 2) + 128)
        qg = jnp.concatenate([qt_ref[0, h * A_HEAD_DIM:(h + 1) * A_HEAD_DIM, :] for h in heads], axis=1)
        zero = jnp.zeros_like(qg)
        qpad = jnp.concatenate([qg, zero] if g % 2 == 0 else [zero, qg], axis=0)
        s_scr[g, 0:nc, :] = jnp.dot(kc_ref[0, :, pair], qpad, preferred_element_type=f32)
        if banded:
            s_scr[g, nc:, :] = jnp.dot(k_ref[0, pl.ds(bs, A_BAND), pair], qpad,
                                       preferred_element_type=f32) + bias

    def softmax(g):
        heads = range(g * A_GROUP, (g + 1) * A_GROUP)
        sk = jnp.concatenate([jnp.full((1, A_BLOCK), sink_ref[h] * LOG2E, f32) for h in heads], axis=1)
        s = s_scr[g]
        m = jnp.maximum(jnp.max(s, axis=0, keepdims=True), sk)
        p = jnp.exp2(s - m)
        inv = 1.0 / (jnp.sum(p, axis=0, keepdims=True) + jnp.exp2(sk - m))
        return p.astype(bf16), inv

    def output(g, pb, inv):
        heads = range(g * A_GROUP, (g + 1) * A_GROUP)
        kvrows = slice(g * A_HEAD_DIM, (g + 1) * A_HEAD_DIM)
        ot = jnp.dot(vct_ref[0, kvrows, :], pb[0:nc], preferred_element_type=f32)
        if banded:
            ot = ot + jnp.dot(vt_ref[0, kvrows, pl.ds(bs, A_BAND)], pb[nc:], preferred_element_type=f32)
        ot = ot * inv
        for a, h in enumerate(heads):
            rows = slice(h * A_HEAD_DIM, (h + 1) * A_HEAD_DIM)
            gate = _silu(zt_ref[0, rows, :].astype(f32))
            ogt_ref[0, rows, :] = (ot[:, a * A_BLOCK:(a + 1) * A_BLOCK] * gate).astype(bf16)

    scores(0)
    scores(1)
    sm0 = softmax(0)
    scores(2)
    sm1 = softmax(1)
    output(0, *sm0)
    scores(3)
    sm2 = softmax(2)
    output(1, *sm1)
    sm3 = softmax(3)
    output(2, *sm2)
    output(3, *sm3)


def _attn(sink, qt, k, vt, kc, vct, zt, banded):
    bsz, _, nq = qt.shape
    nk, nc = k.shape[1], kc.shape[1]
    nkeys = nc + (A_BAND if banded else 0)
    return pl.pallas_call(
        functools.partial(_attn_kernel, banded=banded),
        grid=(bsz, nq // A_BLOCK),
        in_specs=[pl.BlockSpec(memory_space=pltpu.SMEM),
                  pl.BlockSpec((1, A_WIDTH, A_BLOCK), lambda b, i: (b, 0, i)),
                  pl.BlockSpec((1, nk, A_KV_WIDTH), lambda b, i: (b, 0, 0)),
                  pl.BlockSpec((1, A_KV_WIDTH, nk), lambda b, i: (b, 0, 0)),
                  pl.BlockSpec((1, nc, A_KV_WIDTH), lambda b, i: (b, 0, 0)),
                  pl.BlockSpec((1, A_KV_WIDTH, nc), lambda b, i: (b, 0, 0)),
                  pl.BlockSpec((1, A_WIDTH, A_BLOCK), lambda b, i: (b, 0, i))],
        out_specs=pl.BlockSpec((1, A_WIDTH, A_BLOCK), lambda b, i: (b, 0, i)),
        out_shape=jax.ShapeDtypeStruct((bsz, A_WIDTH, nq), bf16),
        scratch_shapes=[pltpu.VMEM((A_KV_HEADS, nkeys, A_GROUP * A_BLOCK), f32)],
        compiler_params=_params("parallel", "parallel"),
        name="attention_banded" if banded else "attention_context",
    )(sink, qt, k, vt, kc, vct, zt)


def _out_c_kernel(at_ref, x_ref, mod_ref, gpost_ref, wo_ref, o_ref):
    y = jnp.dot(at_ref[0].T, wo_ref[...], preferred_element_type=f32)
    o_ref[0] = _gated_residual(x_ref[0], y, gpost_ref[...], mod_ref[0])


def _out_c(at, x, mod, gpost, wo, tm):
    bsz, n, _ = x.shape
    return pl.pallas_call(
        _out_c_kernel,
        grid=(bsz, n // tm),
        in_specs=[pl.BlockSpec((1, A_WIDTH, tm), lambda b, i: (b, 0, i)),
                  pl.BlockSpec((1, tm, D_MODEL), lambda b, i: (b, i, 0)),
                  pl.BlockSpec((1, 1, 3 * D_MODEL), lambda b, i: (b % mod.shape[0], 0, 0)),
                  pl.BlockSpec((1, D_MODEL), lambda b, i: (0, 0)),
                  pl.BlockSpec((A_WIDTH, D_MODEL), lambda b, i: (0, 0))],
        out_specs=pl.BlockSpec((1, tm, D_MODEL), lambda b, i: (b, i, 0)),
        out_shape=jax.ShapeDtypeStruct((bsz, n, D_MODEL), f32),
        compiler_params=_params("parallel", "parallel"),
        name="out_proj_c",
    )(at, x, mod, gpost, wo)


def _rope_tables(n):
    rows = n // GRID_W
    row = jnp.repeat(jnp.arange(rows), GRID_W).astype(f32)
    col = jnp.tile(jnp.arange(GRID_W), rows).astype(f32)
    n_freq = A_HEAD_DIM // 4
    inv = ROPE_BASE ** (-jnp.arange(n_freq, dtype=f32) / n_freq)
    ang = jnp.concatenate([row[:, None] * inv, col[:, None] * inv], -1)
    cos, sin = jnp.cos(ang), jnp.sin(ang)
    cos_t = jnp.tile(jnp.concatenate([cos, cos], -1), (1, 128 // A_HEAD_DIM))
    sin_t = jnp.tile(jnp.concatenate([-sin, sin], -1), (1, 128 // A_HEAD_DIM))
    return cos.T, sin.T, cos_t, sin_t


def _split_pairs_columns(w, heads):
    perm = np.concatenate([np.arange(0, A_HEAD_DIM, 2), np.arange(1, A_HEAD_DIM, 2)])
    idx = (np.arange(heads)[:, None] * A_HEAD_DIM + perm[None, :]).reshape(-1)
    return w[:, idx]


def kernel(x, c, ctx, c_ctx, w_mod, b_mod, g_pre, g_post, ab_w_in, ab_b_gate, ab_conv, ab_mnorm,
           ab_pool_w, ab_pool_scale, ab_w_out, c_w_in, c_sink, c_w_out):
    bsz, n, _ = x.shape
    n_ctx = ctx.shape[1]
    assert bsz < 16
    cs = jnp.zeros((16, D_MODEL), f32).at[:bsz].set(c).at[bsz].set(c_ctx)
    mod_all = _modulation(cs, w_mod, b_mod)
    cost, sint, cos, sin = _rope_tables(n)
    tm_x = 1024

    for l in range(DEPTH):
        j = l // 2
        last = l == DEPTH - 1
        mod_x = mod_all[l, :bsz].reshape(bsz, 1, 3 * D_MODEL)
        mod_c = mod_all[l, bsz:bsz + 1].reshape(1, 1, 3 * D_MODEL)
        gpre = g_pre[l].reshape(1, D_MODEL)
        gpost = g_post[l].reshape(1, D_MODEL)
        if l % 2 == 0:
            w = ab_w_in[j].astype(bf16)
            w_tok = jnp.concatenate([w[:, :2 * M_WIDTH], w[:, 5 * M_WIDTH:GATE_OFF]], axis=1)
            wt_ch = w[:, 2 * M_WIDTH:5 * M_WIDTH].T
            wg = jnp.pad(w[:, GATE_OFF:], ((0, 0), (0, GATE_PAD - N_GATE)))
            bg = jnp.pad(ab_b_gate[j], (0, GATE_PAD - N_GATE)).reshape(1, GATE_PAD)
            k_scale = jnp.concatenate([jnp.ones((M_WIDTH,), f32), jnp.full((M_WIDTH,), M_HEAD_DIM ** -0.5, f32)])
            cw = ab_conv[j] * k_scale
            mnorm_tab = jnp.broadcast_to(ab_mnorm[j][:, None], (M_WIDTH, 128))

            def project(a, mod_a, tm):
                q, k, xp, zp, g = _in_tok(a, mod_a, gpre, w_tok, wg, bg, cw, tm)
                vt, ot, zt = _in_ch(a, mod_a, gpre, wt_ch, tm)
                return (q, k, vt, ot, zt, g), xp, zp

            lat, xpx, zpx = project(x, mod_x, 512)
            con, xpc, zpc = project(ctx, mod_c, n_ctx)
            ymx, ymc = _mlstm(lat, con, mnorm_tab)
            wo = ab_w_out[j].astype(bf16)
            pw = ab_pool_w[j].astype(bf16)
            ps = ab_pool_scale[j].reshape(1, P_WIDTH)
            x = _out_ab(ymx, xpx, zpx, x, mod_x, gpost, wo, pw, ps, 512)
            if not last:
                ctx = _out_ab(ymc, xpc, zpc, ctx, mod_c, gpost, wo, pw, ps, n_ctx)
        else:
            w = c_w_in[j].astype(bf16)
            wt = jnp.concatenate([_split_pairs_columns(w[:, :A_WIDTH], A_HEADS),
                                  w[:, A_WIDTH + A_KV_WIDTH:]], axis=1).T
            wk = _split_pairs_columns(w[:, A_WIDTH:A_WIDTH + A_KV_WIDTH], A_KV_HEADS)
            qx, kx, vx, zx = _in_c(x, mod_x, gpre, wt, wk, cost, sint, cos, sin, 512, True)
            qc, kc, vc, zc = _in_c(ctx, mod_c, gpre, wt, wk, cost[:, :n_ctx], sint[:, :n_ctx],
                                   cos[:n_ctx], sin[:n_ctx], n_ctx, False)
            wo = c_w_out[j].astype(bf16)
            ax = _attn(c_sink[j], qx, kx, vx, kc, vc, zx, True)
            x = _out_c(ax, x, mod_x, gpost, wo, tm_x)
            if not last:
                ac = _attn(c_sink[j], qc, kc, vc, kc, vc, zc, False)
                ctx = _out_c(ac, ctx, mod_c, gpost, wo, n_ctx)
    return x
```

```python
import functools

import jax
import jax.numpy as jnp
import numpy as np
from jax import lax
from jax.experimental import pallas as pl
from jax.experimental.pallas import tpu as pltpu

f32 = jnp.float32
bf16 = jnp.bfloat16

D_MODEL = 1024
DEPTH = 4
EPS = 1e-6
GRID_W = 64

M_HEADS = 4
M_HEAD_DIM = 256
M_WIDTH = 1024
M_CHUNK = 128
P_WINDOWS = (2, 4, 8, 16)
P_GROUP_DIM = 256
P_WIDTH = 1024
N_GATE = 16
GATE_OFF = 5 * M_WIDTH + 2 * P_WIDTH
GATE_PAD = 128

A_HEADS = 16
A_KV_HEADS = 4
A_GROUP = 4
A_HEAD_DIM = 64
A_WIDTH = 1024
A_KV_WIDTH = 256
A_WINDOW = 128
A_BLOCK = 128
A_BAND = A_BLOCK + 2 * A_WINDOW
A_QUERY_BLOCKS = 4
ROPE_BASE = 10000.0

BF16_ROWS = 16
VMEM_LIMIT = 52 * 1024 * 1024


def _params(*sem):
    return pltpu.CompilerParams(dimension_semantics=sem, vmem_limit_bytes=VMEM_LIMIT)


def _silu(a):
    return a * jax.nn.sigmoid(a)


def _modulated_norm(x, gpre, mod):
    xn = x * lax.rsqrt(jnp.mean(x * x, -1, keepdims=True) + EPS) * gpre
    return xn * (1.0 + mod[:, D_MODEL:2 * D_MODEL]) + mod[:, :D_MODEL]


def _gated_residual(x, y, gpost, mod):
    r = y * lax.rsqrt(jnp.mean(y * y, -1, keepdims=True) + EPS) * gpost
    return x + mod[:, 2 * D_MODEL:] * r


def _mod_kernel(cs_ref, w_ref, b_ref, o_ref):
    a = _silu(cs_ref[...])
    o_ref[0] = jnp.dot(a, w_ref[0], preferred_element_type=f32,
                       precision=lax.Precision.HIGHEST) + b_ref[0]


def _modulation(cs, w_mod, b_mod):
    tn = 512
    return pl.pallas_call(
        _mod_kernel,
        grid=(DEPTH, 3 * D_MODEL // tn),
        in_specs=[pl.BlockSpec((16, D_MODEL), lambda l, j: (0, 0)),
                  pl.BlockSpec((1, D_MODEL, tn), lambda l, j: (l, 0, j)),
                  pl.BlockSpec((1, 1, tn), lambda l, j: (l, 0, j))],
        out_specs=pl.BlockSpec((1, 16, tn), lambda l, j: (l, 0, j)),
        out_shape=jax.ShapeDtypeStruct((DEPTH, 16, 3 * D_MODEL), f32),
        compiler_params=_params("parallel", "parallel"),
        name="modulation",
    )(cs, w_mod, b_mod.reshape(DEPTH, 1, 3 * D_MODEL))


NT_DIMS = (((1,), (1,)), ((), ()))
C_ROWS = 512
HALO = 16
CONV_COLS = 256
SUB_ROWS = 256


def _in_tok_kernel(x_ref, xprev_ref, xnext_ref, mod_ref, gpre_ref, w_ref, wg_ref, bg_ref, cw_ref,
                   q_ref, k_ref, xp_ref, zp_ref, g_ref, h_scr):
    i, nt, j = pl.program_id(1), pl.num_programs(1), pl.program_id(2)
    tm = x_ref.shape[1]

    @pl.when(j == 0)
    def _():
        def normed(a):
            return _modulated_norm(a, gpre_ref[...], mod_ref[0]).astype(bf16)

        h_scr[0:HALO, :] = jnp.where(i > 0, normed(xprev_ref[0]), 0.0).astype(bf16)
        h_scr[HALO:HALO + tm, :] = normed(x_ref[0])
        h_scr[HALO + tm:, :] = jnp.where(i < nt - 1, normed(xnext_ref[0]), 0.0).astype(bf16)
        g_ref[0] = jnp.dot(h_scr[HALO:HALO + tm, :], wg_ref[...], preferred_element_type=f32) + bg_ref[...]

        ext = tm + 2 * HALO
        for half, out_ref in enumerate((q_ref, k_ref)):
            for lo in range(0, M_WIDTH, CONV_COLS):
                src = slice(half * M_WIDTH + lo, half * M_WIDTH + lo + CONV_COLS)
                r = jnp.dot(h_scr[...], w_ref[:, src], preferred_element_type=f32)
                conv = (cw_ref[0:1, src] * pltpu.roll(r, 1, 0) + cw_ref[1:2, src] * r
                        + cw_ref[2:3, src] * pltpu.roll(r, ext - 1, 0))
                out_ref[0, :, lo:lo + CONV_COLS] = conv[HALO:HALO + tm].astype(bf16)

    @pl.when(j == 1)
    def _():
        for half, out_ref in enumerate((xp_ref, zp_ref)):
            for lo in range(0, P_WIDTH, CONV_COLS):
                src = slice(half * P_WIDTH + lo, half * P_WIDTH + lo + CONV_COLS)
                out_ref[0, :, lo:lo + CONV_COLS] = jnp.dot(
                    h_scr[HALO:HALO + tm, :], w_ref[:, src], preferred_element_type=f32).astype(bf16)


def _in_tok(x, mod, gpre, w, wg, bg, cw, tm):
    bsz, n, _ = x.shape
    th, nh = tm // HALO, n // HALO
    tok = jax.ShapeDtypeStruct((bsz, n, M_WIDTH), bf16)
    tok_spec = pl.BlockSpec((1, tm, M_WIDTH), lambda b, i, j: (b, i, 0))
    return pl.pallas_call(
        _in_tok_kernel,
        grid=(bsz, n // tm, 2),
        in_specs=[pl.BlockSpec((1, tm, D_MODEL), lambda b, i, j: (b, i, 0)),
                  pl.BlockSpec((1, HALO, D_MODEL), lambda b, i, j: (b, jnp.maximum(i * th - 1, 0), 0)),
                  pl.BlockSpec((1, HALO, D_MODEL), lambda b, i, j: (b, jnp.minimum((i + 1) * th, nh - 1), 0)),
                  pl.BlockSpec((1, 1, 3 * D_MODEL), lambda b, i, j: (b % mod.shape[0], 0, 0)),
                  pl.BlockSpec((1, D_MODEL), lambda b, i, j: (0, 0)),
                  pl.BlockSpec((D_MODEL, 2 * M_WIDTH), lambda b, i, j: (0, j)),
                  pl.BlockSpec((D_MODEL, GATE_PAD), lambda b, i, j: (0, 0)),
                  pl.BlockSpec((1, GATE_PAD), lambda b, i, j: (0, 0)),
                  pl.BlockSpec((3, 2 * M_WIDTH), lambda b, i, j: (0, 0))],
        out_specs=[tok_spec, tok_spec, tok_spec, tok_spec,
                   pl.BlockSpec((1, tm, GATE_PAD), lambda b, i, j: (b, i, 0))],
        out_shape=[tok, tok, tok, tok, jax.ShapeDtypeStruct((bsz, n, GATE_PAD), f32)],
        scratch_shapes=[pltpu.VMEM((tm + 2 * HALO, D_MODEL), bf16)],
        compiler_params=_params("parallel", "parallel", "arbitrary"),
        name="in_proj_ab_tok",
    )(x, x, x, mod, gpre, w, wg, bg, cw)


def _in_ch_kernel(x_ref, mod_ref, gpre_ref, wt_ref, vt_ref, ot_ref, zt_ref):
    for t0 in range(0, x_ref.shape[1], SUB_ROWS):
        toks = slice(t0, t0 + SUB_ROWS)
        hb = _modulated_norm(x_ref[0, toks, :], gpre_ref[...], mod_ref[0]).astype(bf16)
        for idx, out_ref in enumerate((vt_ref, ot_ref, zt_ref)):
            for lo in range(0, M_WIDTH, C_ROWS):
                rows = wt_ref[idx * M_WIDTH + lo:idx * M_WIDTH + lo + C_ROWS, :]
                out_ref[0, lo:lo + C_ROWS, toks] = lax.dot_general(
                    rows, hb, NT_DIMS, preferred_element_type=f32).astype(bf16)


def _in_ch(x, mod, gpre, wt, tm):
    bsz, n, _ = x.shape
    ch = jax.ShapeDtypeStruct((bsz, M_WIDTH, n), bf16)
    ch_spec = pl.BlockSpec((1, M_WIDTH, tm), lambda b, i: (b, 0, i))
    return pl.pallas_call(
        _in_ch_kernel,
        grid=(bsz, n // tm),
        in_specs=[pl.BlockSpec((1, tm, D_MODEL), lambda b, i: (b, i, 0)),
                  pl.BlockSpec((1, 1, 3 * D_MODEL), lambda b, i: (b % mod.shape[0], 0, 0)),
                  pl.BlockSpec((1, D_MODEL), lambda b, i: (0, 0)),
                  pl.BlockSpec(wt.shape, lambda b, i: (0, 0))],
        out_specs=[ch_spec, ch_spec, ch_spec],
        out_shape=[ch, ch, ch],
        compiler_params=_params("parallel", "parallel"),
        name="in_proj_ab_ch",
    )(x, mod, gpre, wt)


def _log_sigmoid(a):
    return jnp.minimum(a, 0.0) - jnp.log1p(jnp.exp(-jnp.abs(a)))


def _gate_rows(g_ref, r_scr):
    g = g_ref[0]
    n = g.shape[0]
    sub = lax.broadcasted_iota(jnp.int32, (N_GATE, GATE_PAD), 0)
    lane = lax.broadcasted_iota(jnp.int32, (N_GATE, GATE_PAD), 1)
    pick = jnp.where(lane == (sub % 4) * M_HEADS + sub // 4, 1.0, 0.0).astype(bf16)
    rows = jnp.zeros((N_GATE, n), f32)
    rest = g
    for _ in range(3):
        piece = rest.astype(bf16)
        rest = rest - piece.astype(f32)
        rows = rows + lax.dot_general(pick, piece, NT_DIMS, preferred_element_type=f32)
    logf = _log_sigmoid(rows)
    pos = lax.broadcasted_iota(jnp.int32, (N_GATE, n), 1) % M_CHUNK
    pre, suf = logf, logf
    shift = 1
    while shift < M_CHUNK:
        pre = pre + jnp.where(pos >= shift, pltpu.roll(pre, shift, 1), 0.0)
        suf = suf + jnp.where(pos < M_CHUNK - shift, pltpu.roll(suf, n - shift, 1), 0.0)
        shift *= 2
    kind = lax.broadcasted_iota(jnp.int32, (N_GATE, n), 0) % 4
    final = jnp.where(kind == 1, pre, jnp.where(kind == 3, suf, rows))
    for h in range(M_HEADS):
        r_scr[h, 0:4, :] = final[4 * h:4 * h + 4]


STATE_ROWS = M_HEAD_DIM + BF16_ROWS


def _mlstm_chunk(seg, consts, state, start, direction, first):
    q_ref, k_ref, vt_ref, ot_ref, zt_ref, r_scr, acc, y_ref = seg
    mn_ref, head = consts
    ct_scr, m_scr = state
    if not isinstance(start, int):
        start = pl.multiple_of(start, M_CHUNK)
    span = pl.ds(start, M_CHUNK)
    dk = M_HEAD_DIM

    q = q_ref[0, span, :]
    k = k_ref[0, span, :]
    vt = vt_ref[0, :, span]

    li = r_scr[head, 2 * direction:2 * direction + 1, span]
    bcum = r_scr[head, 2 * direction + 1:2 * direction + 2, span]
    si = lax.broadcasted_iota(jnp.int32, (M_CHUNK, M_CHUNK), 0)
    ti = lax.broadcasted_iota(jnp.int32, (M_CHUNK, M_CHUNK), 1)
    causal = (si <= ti) if direction == 0 else (si >= ti)
    b_last = bcum[:, M_CHUNK - 1:M_CHUNK] if direction == 0 else bcum[:, 0:1]

    m_prev = m_scr[direction]
    ct = ct_scr[direction]
    n_mem = ct[dk:dk + 1, :]

    key_side = jnp.transpose(jnp.broadcast_to(li - bcum, (M_CHUNK, M_CHUNK)))
    dmat = jnp.where(causal, bcum + key_side, -jnp.inf)
    inter = bcum + m_prev
    m_t = jnp.maximum(inter, jnp.max(dmat, axis=0, keepdims=True))
    w_inter = jnp.exp(inter - m_t)
    s = lax.dot_general(k, q, NT_DIMS, preferred_element_type=f32) * jnp.exp(dmat - m_t)

    tail = lax.broadcasted_iota(jnp.int32, (BF16_ROWS, dk), 0)
    n_hi = n_mem.astype(bf16).astype(f32)
    extra = jnp.where(tail == 0, n_hi, jnp.where(tail == 1, n_mem - n_hi, 0.0))
    lhs = jnp.concatenate([ct[:dk].astype(bf16), extra.astype(bf16)], axis=0)
    from_state = lax.dot_general(lhs, q, NT_DIMS, preferred_element_type=f32)
    num = w_inter * from_state[:dk] + jnp.dot(vt, s.astype(bf16), preferred_element_type=f32)
    den = (w_inter * (from_state[dk:dk + 1] + from_state[dk + 1:dk + 2])
           + jnp.sum(s, axis=0, keepdims=True))
    h = num * (1.0 / jnp.maximum(jnp.abs(den), jnp.exp(-m_t)))

    d_last = b_last - bcum + li
    m_new = jnp.maximum(b_last + m_prev, jnp.max(d_last, axis=-1, keepdims=True))
    w_row = jnp.exp(d_last - m_new)
    decay = jnp.exp(b_last + m_prev - m_new)
    w_hi = w_row.astype(bf16).astype(f32)
    w_extra = jnp.where(tail[:, :M_CHUNK] == 0, w_hi, jnp.where(tail[:, :M_CHUNK] == 1, w_row - w_hi, 0.0))
    va = jnp.concatenate([(vt.astype(f32) * w_row).astype(bf16), w_extra.astype(bf16)], axis=0)
    update = jnp.dot(va, k, preferred_element_type=f32)
    ct_scr[direction, 0:dk, :] = decay * ct[:dk] + update[:dk]
    n_new = decay * n_mem + update[dk:dk + 1] + update[dk + 1:dk + 2]
    ct_scr[direction, dk:, :] = jnp.where(tail == 0, n_new, 0.0)
    m_scr[direction] = m_new

    if first:
        acc[:, span] = h
    else:
        hg = (acc[:, span] + h) * jax.nn.sigmoid(ot_ref[0, :, span].astype(f32))
        hn = hg * lax.rsqrt(jnp.mean(hg * hg, axis=0, keepdims=True) + EPS) * mn_ref[...]
        y_ref[0, :, span] = (hn * _silu(zt_ref[0, :, span].astype(f32))).astype(bf16)


def _mlstm_kernel(qx_ref, kx_ref, vx_ref, ox_ref, zx_ref, gx_ref,
                  qc_ref, kc_ref, vc_ref, oc_ref, zc_ref, gc_ref,
                  mn_ref, yx_ref, yc_ref,
                  rx_scr, rc_scr, accx, accc, ct_scr, m_scr):
    head = pl.program_id(1)
    nx, nc = qx_ref.shape[1], qc_ref.shape[1]

    @pl.when(head == 0)
    def _():
        _gate_rows(gx_ref, rx_scr)
        _gate_rows(gc_ref, rc_scr)

    ct_scr[...] = jnp.zeros_like(ct_scr)
    m_scr[...] = jnp.zeros_like(m_scr)

    consts = (mn_ref, head)
    state = (ct_scr, m_scr)
    ctx = (qc_ref, kc_ref, vc_ref, oc_ref, zc_ref, rc_scr, accc, yc_ref)
    lat = (qx_ref, kx_ref, vx_ref, ox_ref, zx_ref, rx_scr, accx, yx_ref)
    step = functools.partial(_mlstm_chunk, consts=consts, state=state)

    ncc = nc // M_CHUNK
    assert ncc == 2
    step(ctx, start=0, direction=0, first=True)
    step(ctx, start=M_CHUNK, direction=1, first=True)
    step(ctx, start=M_CHUNK, direction=0, first=False)
    step(ctx, start=0, direction=1, first=False)

    nxc = nx // M_CHUNK

    def meet(first):
        def body(i, carry):
            step(lat, start=i * M_CHUNK, direction=0, first=first)
            step(lat, start=(nxc - 1 - i) * M_CHUNK, direction=1, first=first)
            return carry
        return body

    lax.fori_loop(0, nxc // 2, meet(True), 0, unroll=8)
    lax.fori_loop(nxc // 2, nxc, meet(False), 0, unroll=4)


def _mlstm(lat, ctx, mnorm_tab):
    bsz, nx, _ = lat[0].shape
    nc = ctx[0].shape[1]
    hd = M_HEAD_DIM

    def specs(n):
        tok = pl.BlockSpec((1, n, hd), lambda b, h: (b, 0, h))
        ch = pl.BlockSpec((1, hd, n), lambda b, h: (b, h, 0))
        return [tok, tok, ch, ch, ch, pl.BlockSpec((1, n, GATE_PAD), lambda b, h: (b, 0, 0))]

    return pl.pallas_call(
        _mlstm_kernel,
        grid=(bsz, M_HEADS),
        in_specs=specs(nx) + specs(nc) + [pl.BlockSpec((hd, 128), lambda b, h: (h, 0))],
        out_specs=[pl.BlockSpec((1, hd, nx), lambda b, h: (b, h, 0)),
                   pl.BlockSpec((1, hd, nc), lambda b, h: (b, h, 0))],
        out_shape=[jax.ShapeDtypeStruct((bsz, M_WIDTH, nx), bf16),
                   jax.ShapeDtypeStruct((bsz, M_WIDTH, nc), bf16)],
        scratch_shapes=[pltpu.VMEM((M_HEADS, 8, nx), f32), pltpu.VMEM((M_HEADS, 8, nc), f32),
                        pltpu.VMEM((hd, nx), f32), pltpu.VMEM((hd, nc), f32),
                        pltpu.VMEM((2, STATE_ROWS, hd), f32), pltpu.VMEM((2, 1, 1), f32)],
        compiler_params=_params("parallel", "arbitrary"),
        name="mlstm",
    )(*lat, *ctx, mnorm_tab)


P_HALO = 16


def _out_ab_kernel(ym_ref, xp_ref, xpp_ref, xpn_ref, zp_ref, x_ref, mod_ref, gpost_ref,
                   wo_ref, pw_ref, ps_ref, o_ref, ext_scr):
    i, nt = pl.program_id(1), pl.num_programs(1)
    tm = x_ref.shape[1]
    ext_scr[0:P_HALO, :] = jnp.where(i > 0, xpp_ref[0].astype(f32), 0.0)
    ext_scr[P_HALO:P_HALO + tm, :] = xp_ref[0].astype(f32)
    ext_scr[P_HALO + tm:, :] = jnp.where(i < nt - 1, xpn_ref[0].astype(f32), 0.0)

    n = nt * tm
    sub = min(SUB_ROWS, tm)
    ext_rows = sub + 2 * P_HALO
    for r0 in range(0, tm, sub):
        rows = slice(r0, r0 + sub)
        t = i * tm + r0 + lax.broadcasted_iota(jnp.int32, (sub, 1), 0)
        y = jnp.dot(ym_ref[0, :, rows].T, wo_ref[0:M_WIDTH, :], preferred_element_type=f32)
        for g, win in enumerate(P_WINDOWS):
            half = win // 2
            cols = slice(g * P_GROUP_DIM, (g + 1) * P_GROUP_DIM)
            e = ext_scr[r0:r0 + ext_rows, cols]
            run, width = e, 1
            while width < win:
                run = run + pltpu.roll(run, ext_rows - width, 0)
                width *= 2
            total = pltpu.roll(run, half, 0)[P_HALO:P_HALO + sub]
            count = (jnp.minimum(t + half, n) - jnp.maximum(t - half, 0)).astype(f32)
            pooled = total * (1.0 / count) - e[P_HALO:P_HALO + sub]
            mixed = jnp.dot(pooled.astype(bf16), pw_ref[g], preferred_element_type=f32) * ps_ref[:, cols]
            yp = (mixed * _silu(zp_ref[0, rows, cols].astype(f32))).astype(bf16)
            y = y + jnp.dot(yp, wo_ref[M_WIDTH + g * P_GROUP_DIM:M_WIDTH + (g + 1) * P_GROUP_DIM, :],
                            preferred_element_type=f32)
        o_ref[0, rows, :] = _gated_residual(x_ref[0, rows, :], y, gpost_ref[...], mod_ref[0])


def _out_ab(ymt, xp, zp, x, mod, gpost, wo, pw, ps, tm):
    bsz, n, _ = x.shape
    nh = n // P_HALO
    th = tm // P_HALO
    return pl.pallas_call(
        _out_ab_kernel,
        grid=(bsz, n // tm),
        in_specs=[pl.BlockSpec((1, M_WIDTH, tm), lambda b, i: (b, 0, i)),
                  pl.BlockSpec((1, tm, P_WIDTH), lambda b, i: (b, i, 0)),
                  pl.BlockSpec((1, P_HALO, P_WIDTH), lambda b, i: (b, jnp.maximum(i * th - 1, 0), 0)),
                  pl.BlockSpec((1, P_HALO, P_WIDTH), lambda b, i: (b, jnp.minimum((i + 1) * th, nh - 1), 0)),
                  pl.BlockSpec((1, tm, P_WIDTH), lambda b, i: (b, i, 0)),
                  pl.BlockSpec((1, tm, D_MODEL), lambda b, i: (b, i, 0)),
                  pl.BlockSpec((1, 1, 3 * D_MODEL), lambda b, i: (b % mod.shape[0], 0, 0)),
                  pl.BlockSpec((1, D_MODEL), lambda b, i: (0, 0)),
                  pl.BlockSpec((M_WIDTH + P_WIDTH, D_MODEL), lambda b, i: (0, 0)),
                  pl.BlockSpec((len(P_WINDOWS), P_GROUP_DIM, P_GROUP_DIM), lambda b, i: (0, 0, 0)),
                  pl.BlockSpec((1, P_WIDTH), lambda b, i: (0, 0))],
        out_specs=pl.BlockSpec((1, tm, D_MODEL), lambda b, i: (b, i, 0)),
        out_shape=jax.ShapeDtypeStruct((bsz, n, D_MODEL), f32),
        scratch_shapes=[pltpu.VMEM((tm + 2 * P_HALO, P_WIDTH), f32)],
        compiler_params=_params("parallel", "parallel"),
        name="out_proj_ab",
    )(ymt, xp, xp, xp, zp, x, mod, gpost, wo, pw, ps)


def _rope(a, cos, sin):
    lane = lax.broadcasted_iota(jnp.int32, a.shape, 1)
    half = A_HEAD_DIM // 2
    partner = jnp.where(lane % A_HEAD_DIM < half,
                        pltpu.roll(a, a.shape[1] - half, 1), pltpu.roll(a, half, 1))
    return a * cos + partner * sin


LOG2E = 1.4426950408889634


def _in_c_kernel(x_ref, mod_ref, gpre_ref, wt_ref, wk_ref, cost_ref, sint_ref, cos_ref, sin_ref,
                 qt_ref, k_ref, vt_ref, zt_ref, *, rope):
    half = A_HEAD_DIM // 2
    q_scale = (A_HEAD_DIM ** -0.5) * LOG2E
    for t0 in range(0, x_ref.shape[1], SUB_ROWS):
        toks = slice(t0, t0 + SUB_ROWS)
        hb = _modulated_norm(x_ref[0, toks, :], gpre_ref[...], mod_ref[0]).astype(bf16)

        def channel_major(lo, rows):
            return lax.dot_general(wt_ref[lo:lo + rows, :], hb, NT_DIMS, preferred_element_type=f32)

        for lo in range(0, A_WIDTH, C_ROWS):
            a = channel_major(lo, C_ROWS)
            for h in range(C_ROWS // A_HEAD_DIM):
                x1 = a[h * A_HEAD_DIM:h * A_HEAD_DIM + half]
                x2 = a[h * A_HEAD_DIM + half:(h + 1) * A_HEAD_DIM]
                if rope:
                    c, s = cost_ref[:, toks], sint_ref[:, toks]
                    x1, x2 = x1 * c - x2 * s, x1 * s + x2 * c
                r0 = lo + h * A_HEAD_DIM
                qt_ref[0, r0:r0 + half, toks] = (x1 * q_scale).astype(bf16)
                qt_ref[0, r0 + half:r0 + A_HEAD_DIM, toks] = (x2 * q_scale).astype(bf16)
        vt_ref[0, :, toks] = channel_major(A_WIDTH, A_KV_WIDTH).astype(bf16)
        for lo in range(0, A_WIDTH, C_ROWS):
            zt_ref[0, lo:lo + C_ROWS, toks] = channel_major(A_WIDTH + A_KV_WIDTH + lo, C_ROWS).astype(bf16)
        for j in range(A_KV_WIDTH // 128):
            kk = jnp.dot(hb, wk_ref[:, j * 128:(j + 1) * 128], preferred_element_type=f32)
            if rope:
                kk = _rope(kk, cos_ref[toks, :], sin_ref[toks, :])
            k_ref[0, toks, j * 128:(j + 1) * 128] = kk.astype(bf16)


def _in_c(x, mod, gpre, wt, wk, cost, sint, cos, sin, tm, rope):
    bsz, n, _ = x.shape
    half = A_HEAD_DIM // 2
    return pl.pallas_call(
        functools.partial(_in_c_kernel, rope=rope),
        grid=(bsz, n // tm),
        in_specs=[pl.BlockSpec((1, tm, D_MODEL), lambda b, i: (b, i, 0)),
                  pl.BlockSpec((1, 1, 3 * D_MODEL), lambda b, i: (b % mod.shape[0], 0, 0)),
                  pl.BlockSpec((1, D_MODEL), lambda b, i: (0, 0)),
                  pl.BlockSpec(wt.shape, lambda b, i: (0, 0)),
                  pl.BlockSpec(wk.shape, lambda b, i: (0, 0)),
                  pl.BlockSpec((half, tm), lambda b, i: (0, i)),
                  pl.BlockSpec((half, tm), lambda b, i: (0, i)),
                  pl.BlockSpec((tm, 128), lambda b, i: (i, 0)),
                  pl.BlockSpec((tm, 128), lambda b, i: (i, 0))],
        out_specs=[pl.BlockSpec((1, A_WIDTH, tm), lambda b, i: (b, 0, i)),
                   pl.BlockSpec((1, tm, A_KV_WIDTH), lambda b, i: (b, i, 0)),
                   pl.BlockSpec((1, A_KV_WIDTH, tm), lambda b, i: (b, 0, i)),
                   pl.BlockSpec((1, A_WIDTH, tm), lambda b, i: (b, 0, i))],
        out_shape=[jax.ShapeDtypeStruct((bsz, A_WIDTH, n), bf16),
                   jax.ShapeDtypeStruct((bsz, n, A_KV_WIDTH), bf16),
                   jax.ShapeDtypeStruct((bsz, A_KV_WIDTH, n), bf16),
                   jax.ShapeDtypeStruct((bsz, A_WIDTH, n), bf16)],
        compiler_params=_params("parallel", "parallel"),
        name="in_proj_c",
    )(x, mod, gpre, wt, wk, cost, sint, cos, sin)


def _attn_kernel(sink_ref, qt_ref, k_ref, vt_ref, kc_ref, vct_ref, zt_ref, ogt_ref, s_scr, *, banded):
    nc = kc_ref.shape[1]
    blocks = qt_ref.shape[2] // A_BLOCK
    band_start, bias = [], []
    if banded:
        n = k_ref.shape[1]
        for qb in range(blocks):
            start = (pl.program_id(1) * blocks + qb) * A_BLOCK
            bs = pl.multiple_of(jnp.clip(start - A_WINDOW, 0, n - A_BAND), A_BLOCK)
            kj = bs + lax.broadcasted_iota(jnp.int32, (A_BAND, A_BLOCK), 0)
            qi = start + lax.broadcasted_iota(jnp.int32, (A_BAND, A_BLOCK), 1)
            mask = jnp.where(jnp.abs(qi - kj) <= A_WINDOW, 0.0, -jnp.inf)
            band_start.append(bs)
            bias.append(jnp.concatenate([mask] * A_GROUP, axis=1))

    def scores(u):
        qb, g = divmod(u, A_KV_HEADS)
        lanes = slice(qb * A_BLOCK, (qb + 1) * A_BLOCK)
        heads = range(g * A_GROUP, (g + 1) * A_GROUP)
        pair = slice(128 * (g // 2), 128 * (g // 2) + 128)
        qg = jnp.concatenate([qt_ref[0, h * A_HEAD_DIM:(h + 1) * A_HEAD_DIM, lanes] for h in heads], axis=1)
        zero = jnp.zeros_like(qg)
        qpad = jnp.concatenate([qg, zero] if g % 2 == 0 else [zero, qg], axis=0)
        slot = u % s_scr.shape[0]
        s_scr[slot, 0:nc, :] = jnp.dot(kc_ref[0, :, pair], qpad, preferred_element_type=f32)
        if banded:
            s_scr[slot, nc:, :] = jnp.dot(k_ref[0, pl.ds(band_start[qb], A_BAND), pair], qpad,
                                       preferred_element_type=f32) + bias[qb]

    def softmax(u):
        g = u % A_KV_HEADS
        heads = range(g * A_GROUP, (g + 1) * A_GROUP)
        sk = jnp.concatenate([jnp.full((1, A_BLOCK), sink_ref[h] * LOG2E, f32) for h in heads], axis=1)
        s = s_scr[u % s_scr.shape[0]]
        m = jnp.maximum(jnp.max(s, axis=0, keepdims=True), sk)
        p = jnp.exp2(s - m)
        inv = 1.0 / (jnp.sum(p, axis=0, keepdims=True) + jnp.exp2(sk - m))
        return p.astype(bf16), inv

    def output(u, pb, inv):
        qb, g = divmod(u, A_KV_HEADS)
        lanes = slice(qb * A_BLOCK, (qb + 1) * A_BLOCK)
        heads = range(g * A_GROUP, (g + 1) * A_GROUP)
        kvrows = slice(g * A_HEAD_DIM, (g + 1) * A_HEAD_DIM)
        ot = jnp.dot(vct_ref[0, kvrows, :], pb[0:nc], preferred_element_type=f32)
        if banded:
            ot = ot + jnp.dot(vt_ref[0, kvrows, pl.ds(band_start[qb], A_BAND)], pb[nc:],
                              preferred_element_type=f32)
        ot = ot * inv
        for a, h in enumerate(heads):
            rows = slice(h * A_HEAD_DIM, (h + 1) * A_HEAD_DIM)
            gate = _silu(zt_ref[0, rows, lanes].astype(f32))
            ogt_ref[0, rows, lanes] = (ot[:, a * A_BLOCK:(a + 1) * A_BLOCK] * gate).astype(bf16)

    units = blocks * A_KV_HEADS
    scores(0)
    scores(1)
    pending = None
    for u in range(units):
        sm = softmax(u)
        if u + 2 < units:
            scores(u + 2)
        if pending is not None:
            output(u - 1, *pending)
        pending = sm
    output(units - 1, *pending)


def _attn(sink, qt, k, vt, kc, vct, zt, banded):
    bsz, _, nq = qt.shape
    nk, nc = k.shape[1], kc.shape[1]
    nkeys = nc + (A_BAND if banded else 0)
    blocks = min(A_QUERY_BLOCKS, nq // A_BLOCK)
    tq = blocks * A_BLOCK
    return pl.pallas_call(
        functools.partial(_attn_kernel, banded=banded),
        grid=(bsz, nq // tq),
        in_specs=[pl.BlockSpec(memory_space=pltpu.SMEM),
                  pl.BlockSpec((1, A_WIDTH, tq), lambda b, i: (b, 0, i)),
                  pl.BlockSpec((1, nk, A_KV_WIDTH), lambda b, i: (b, 0, 0)),
                  pl.BlockSpec((1, A_KV_WIDTH, nk), lambda b, i: (b, 0, 0)),
                  pl.BlockSpec((1, nc, A_KV_WIDTH), lambda b, i: (b, 0, 0)),
                  pl.BlockSpec((1, A_KV_WIDTH, nc), lambda b, i: (b, 0, 0)),
                  pl.BlockSpec((1, A_WIDTH, tq), lambda b, i: (b, 0, i))],
        out_specs=pl.BlockSpec((1, A_WIDTH, tq), lambda b, i: (b, 0, i)),
        out_shape=jax.ShapeDtypeStruct((bsz, A_WIDTH, nq), bf16),
        scratch_shapes=[pltpu.VMEM((4, nkeys, A_GROUP * A_BLOCK), f32)],
        compiler_params=_params("parallel", "parallel"),
        name="attention_banded" if banded else "attention_context",
    )(sink, qt, k, vt, kc, vct, zt)


def _out_c_kernel(at_ref, x_ref, mod_ref, gpost_ref, wo_ref, o_ref):
    for lo in range(0, x_ref.shape[1], SUB_ROWS):
        rows = slice(lo, lo + SUB_ROWS)
        y = jnp.dot(at_ref[0, :, rows].T, wo_ref[...], preferred_element_type=f32)
        o_ref[0, rows, :] = _gated_residual(x_ref[0, rows, :], y, gpost_ref[...], mod_ref[0])


def _out_c(at, x, mod, gpost, wo, tm):
    bsz, n, _ = x.shape
    return pl.pallas_call(
        _out_c_kernel,
        grid=(bsz, n // tm),
        in_specs=[pl.BlockSpec((1, A_WIDTH, tm), lambda b, i: (b, 0, i)),
                  pl.BlockSpec((1, tm, D_MODEL), lambda b, i: (b, i, 0)),
                  pl.BlockSpec((1, 1, 3 * D_MODEL), lambda b, i: (b % mod.shape[0], 0, 0)),
                  pl.BlockSpec((1, D_MODEL), lambda b, i: (0, 0)),
                  pl.BlockSpec((A_WIDTH, D_MODEL), lambda b, i: (0, 0))],
        out_specs=pl.BlockSpec((1, tm, D_MODEL), lambda b, i: (b, i, 0)),
        out_shape=jax.ShapeDtypeStruct((bsz, n, D_MODEL), f32),
        compiler_params=_params("parallel", "parallel"),
        name="out_proj_c",
    )(at, x, mod, gpost, wo)


def _rope_tables(n):
    rows = n // GRID_W
    row = jnp.repeat(jnp.arange(rows), GRID_W).astype(f32)
    col = jnp.tile(jnp.arange(GRID_W), rows).astype(f32)
    n_freq = A_HEAD_DIM // 4
    inv = ROPE_BASE ** (-jnp.arange(n_freq, dtype=f32) / n_freq)
    ang = jnp.concatenate([row[:, None] * inv, col[:, None] * inv], -1)
    cos, sin = jnp.cos(ang), jnp.sin(ang)
    cos_t = jnp.tile(jnp.concatenate([cos, cos], -1), (1, 128 // A_HEAD_DIM))
    sin_t = jnp.tile(jnp.concatenate([-sin, sin], -1), (1, 128 // A_HEAD_DIM))
    return cos.T, sin.T, cos_t, sin_t


def _split_pairs_columns(w, heads):
    perm = np.concatenate([np.arange(0, A_HEAD_DIM, 2), np.arange(1, A_HEAD_DIM, 2)])
    idx = (np.arange(heads)[:, None] * A_HEAD_DIM + perm[None, :]).reshape(-1)
    return w[:, idx]


def kernel(x, c, ctx, c_ctx, w_mod, b_mod, g_pre, g_post, ab_w_in, ab_b_gate, ab_conv, ab_mnorm,
           ab_pool_w, ab_pool_scale, ab_w_out, c_w_in, c_sink, c_w_out):
    bsz, n, _ = x.shape
    n_ctx = ctx.shape[1]
    assert bsz < 16
    cs = jnp.zeros((16, D_MODEL), f32).at[:bsz].set(c).at[bsz].set(c_ctx)
    mod_all = _modulation(cs, w_mod, b_mod)
    cost, sint, cos, sin = _rope_tables(n)
    tm_x = 1024

    for l in range(DEPTH):
        j = l // 2
        last = l == DEPTH - 1
        mod_x = mod_all[l, :bsz].reshape(bsz, 1, 3 * D_MODEL)
        mod_c = mod_all[l, bsz:bsz + 1].reshape(1, 1, 3 * D_MODEL)
        gpre = g_pre[l].reshape(1, D_MODEL)
        gpost = g_post[l].reshape(1, D_MODEL)
        if l % 2 == 0:
            w = ab_w_in[j].astype(bf16)
            w_tok = jnp.concatenate([w[:, :2 * M_WIDTH], w[:, 5 * M_WIDTH:GATE_OFF]], axis=1)
            wt_ch = w[:, 2 * M_WIDTH:5 * M_WIDTH].T
            wg = jnp.pad(w[:, GATE_OFF:], ((0, 0), (0, GATE_PAD - N_GATE)))
            bg = jnp.pad(ab_b_gate[j], (0, GATE_PAD - N_GATE)).reshape(1, GATE_PAD)
            k_scale = jnp.concatenate([jnp.ones((M_WIDTH,), f32), jnp.full((M_WIDTH,), M_HEAD_DIM ** -0.5, f32)])
            cw = ab_conv[j] * k_scale
            mnorm_tab = jnp.broadcast_to(ab_mnorm[j][:, None], (M_WIDTH, 128))

            def project(a, mod_a, tm):
                q, k, xp, zp, g = _in_tok(a, mod_a, gpre, w_tok, wg, bg, cw, tm)
                vt, ot, zt = _in_ch(a, mod_a, gpre, wt_ch, tm)
                return (q, k, vt, ot, zt, g), xp, zp

            lat, xpx, zpx = project(x, mod_x, 512)
            con, xpc, zpc = project(ctx, mod_c, n_ctx)
            ymx, ymc = _mlstm(lat, con, mnorm_tab)
            wo = ab_w_out[j].astype(bf16)
            pw = ab_pool_w[j].astype(bf16)
            ps = ab_pool_scale[j].reshape(1, P_WIDTH)
            x = _out_ab(ymx, xpx, zpx, x, mod_x, gpost, wo, pw, ps, 512)
            if not last:
                ctx = _out_ab(ymc, xpc, zpc, ctx, mod_c, gpost, wo, pw, ps, n_ctx)
        else:
            w = c_w_in[j].astype(bf16)
            wt = jnp.concatenate([_split_pairs_columns(w[:, :A_WIDTH], A_HEADS),
                                  w[:, A_WIDTH + A_KV_WIDTH:]], axis=1).T
            wk = _split_pairs_columns(w[:, A_WIDTH:A_WIDTH + A_KV_WIDTH], A_KV_HEADS)
            qx, kx, vx, zx = _in_c(x, mod_x, gpre, wt, wk, cost, sint, cos, sin, 512, True)
            qc, kc, vc, zc = _in_c(ctx, mod_c, gpre, wt, wk, cost[:, :n_ctx], sint[:, :n_ctx],
                                   cos[:n_ctx], sin[:n_ctx], n_ctx, False)
            wo = c_w_out[j].astype(bf16)
            ax = _attn(c_sink[j], qx, kx, vx, kc, vc, zx, True)
            x = _out_c(ax, x, mod_x, gpost, wo, tm_x)
            if not last:
                ac = _attn(c_sink[j], qc, kc, vc, kc, vc, zc, False)
                ctx = _out_c(ac, ctx, mod_c, gpost, wo, n_ctx)
    return x
```

```python
import functools

import jax
import jax.numpy as jnp
import numpy as np
from jax import lax
from jax.experimental import pallas as pl
from jax.experimental.pallas import tpu as pltpu

f32 = jnp.float32
bf16 = jnp.bfloat16

D_MODEL = 1024
DEPTH = 4
EPS = 1e-6
GRID_W = 64

M_HEADS = 4
M_HEAD_DIM = 256
M_WIDTH = 1024
M_CHUNK = 128
P_WINDOWS = (2, 4, 8, 16)
P_GROUP_DIM = 256
P_WIDTH = 1024
N_GATE = 16
GATE_OFF = 5 * M_WIDTH + 2 * P_WIDTH
GATE_PAD = 128

A_HEADS = 16
A_KV_HEADS = 4
A_GROUP = 4
A_HEAD_DIM = 64
A_WIDTH = 1024
A_KV_WIDTH = 256
A_WINDOW = 128
A_BLOCK = 128
A_BAND = A_BLOCK + 2 * A_WINDOW
A_QUERY_BLOCKS = 4
ROPE_BASE = 10000.0

BF16_ROWS = 16
VMEM_LIMIT = 52 * 1024 * 1024


def _params(*sem):
    return pltpu.CompilerParams(dimension_semantics=sem, vmem_limit_bytes=VMEM_LIMIT)


def _silu(a):
    return a * jax.nn.sigmoid(a)


def _modulated_norm(x, gpre, mod):
    xn = x * lax.rsqrt(jnp.mean(x * x, -1, keepdims=True) + EPS) * gpre
    return xn * (1.0 + mod[:, D_MODEL:2 * D_MODEL]) + mod[:, :D_MODEL]


def _gated_residual(x, y, gpost, mod):
    r = y * lax.rsqrt(jnp.mean(y * y, -1, keepdims=True) + EPS) * gpost
    return x + mod[:, 2 * D_MODEL:] * r


def _mod_kernel(cs_ref, w_ref, b_ref, o_ref):
    a = _silu(cs_ref[...])
    o_ref[0] = jnp.dot(a, w_ref[0], preferred_element_type=f32,
                       precision=lax.Precision.HIGHEST) + b_ref[0]


def _modulation(cs, w_mod, b_mod):
    tn = 1024
    return pl.pallas_call(
        _mod_kernel,
        grid=(DEPTH, 3 * D_MODEL // tn),
        in_specs=[pl.BlockSpec((16, D_MODEL), lambda l, j: (0, 0)),
                  pl.BlockSpec((1, D_MODEL, tn), lambda l, j: (l, 0, j)),
                  pl.BlockSpec((1, 1, tn), lambda l, j: (l, 0, j))],
        out_specs=pl.BlockSpec((1, 16, tn), lambda l, j: (l, 0, j)),
        out_shape=jax.ShapeDtypeStruct((DEPTH, 16, 3 * D_MODEL), f32),
        compiler_params=_params("parallel", "parallel"),
        name="modulation",
    )(cs, w_mod, b_mod.reshape(DEPTH, 1, 3 * D_MODEL))


NT_DIMS = (((1,), (1,)), ((), ()))
C_ROWS = 512
HALO = 16
CONV_COLS = 256
SUB_ROWS = 256


def _in_tok_kernel(x_ref, xprev_ref, xnext_ref, mod_ref, gpre_ref, w_ref, wg_ref, bg_ref, cw_ref,
                   q_ref, k_ref, xp_ref, zp_ref, g_ref, h_scr):
    i, nt, j = pl.program_id(1), pl.num_programs(1), pl.program_id(2)
    tm = x_ref.shape[1]

    @pl.when(j == 0)
    def _():
        def normed(a):
            return _modulated_norm(a, gpre_ref[...], mod_ref[0]).astype(bf16)

        h_scr[0:HALO, :] = jnp.where(i > 0, normed(xprev_ref[0]), 0.0).astype(bf16)
        h_scr[HALO:HALO + tm, :] = normed(x_ref[0])
        h_scr[HALO + tm:, :] = jnp.where(i < nt - 1, normed(xnext_ref[0]), 0.0).astype(bf16)
        g_ref[0] = jnp.dot(h_scr[HALO:HALO + tm, :], wg_ref[...], preferred_element_type=f32) + bg_ref[...]

        ext = tm + 2 * HALO
        for half, out_ref in enumerate((q_ref, k_ref)):
            for lo in range(0, M_WIDTH, CONV_COLS):
                src = slice(half * M_WIDTH + lo, half * M_WIDTH + lo + CONV_COLS)
                r = jnp.dot(h_scr[...], w_ref[:, src], preferred_element_type=f32)
                conv = (cw_ref[0:1, src] * pltpu.roll(r, 1, 0) + cw_ref[1:2, src] * r
                        + cw_ref[2:3, src] * pltpu.roll(r, ext - 1, 0))
                out_ref[0, :, lo:lo + CONV_COLS] = conv[HALO:HALO + tm].astype(bf16)

    @pl.when(j == 1)
    def _():
        for half, out_ref in enumerate((xp_ref, zp_ref)):
            for lo in range(0, P_WIDTH, CONV_COLS):
                src = slice(half * P_WIDTH + lo, half * P_WIDTH + lo + CONV_COLS)
                out_ref[0, :, lo:lo + CONV_COLS] = jnp.dot(
                    h_scr[HALO:HALO + tm, :], w_ref[:, src], preferred_element_type=f32).astype(bf16)


def _in_tok(x, mod, gpre, w, wg, bg, cw, tm):
    bsz, n, _ = x.shape
    th, nh = tm // HALO, n // HALO
    tok = jax.ShapeDtypeStruct((bsz, n, M_WIDTH), bf16)
    tok_spec = pl.BlockSpec((1, tm, M_WIDTH), lambda b, i, j: (b, i, 0))
    return pl.pallas_call(
        _in_tok_kernel,
        grid=(bsz, n // tm, 2),
        in_specs=[pl.BlockSpec((1, tm, D_MODEL), lambda b, i, j: (b, i, 0)),
                  pl.BlockSpec((1, HALO, D_MODEL), lambda b, i, j: (b, jnp.maximum(i * th - 1, 0), 0)),
                  pl.BlockSpec((1, HALO, D_MODEL), lambda b, i, j: (b, jnp.minimum((i + 1) * th, nh - 1), 0)),
                  pl.BlockSpec((1, 1, 3 * D_MODEL), lambda b, i, j: (b % mod.shape[0], 0, 0)),
                  pl.BlockSpec((1, D_MODEL), lambda b, i, j: (0, 0)),
                  pl.BlockSpec((D_MODEL, 2 * M_WIDTH), lambda b, i, j: (0, j)),
                  pl.BlockSpec((D_MODEL, GATE_PAD), lambda b, i, j: (0, 0)),
                  pl.BlockSpec((1, GATE_PAD), lambda b, i, j: (0, 0)),
                  pl.BlockSpec((3, 2 * M_WIDTH), lambda b, i, j: (0, 0))],
        out_specs=[tok_spec, tok_spec, tok_spec, tok_spec,
                   pl.BlockSpec((1, tm, GATE_PAD), lambda b, i, j: (b, i, 0))],
        out_shape=[tok, tok, tok, tok, jax.ShapeDtypeStruct((bsz, n, GATE_PAD), f32)],
        scratch_shapes=[pltpu.VMEM((tm + 2 * HALO, D_MODEL), bf16)],
        compiler_params=_params("parallel", "parallel", "arbitrary"),
        name="in_proj_ab_tok",
    )(x, x, x, mod, gpre, w, wg, bg, cw)


def _in_ch_kernel(x_ref, mod_ref, gpre_ref, wt_ref, vt_ref, ot_ref, zt_ref):
    for t0 in range(0, x_ref.shape[1], SUB_ROWS):
        toks = slice(t0, t0 + SUB_ROWS)
        hb = _modulated_norm(x_ref[0, toks, :], gpre_ref[...], mod_ref[0]).astype(bf16)
        for idx, out_ref in enumerate((vt_ref, ot_ref, zt_ref)):
            for lo in range(0, M_WIDTH, C_ROWS):
                rows = wt_ref[idx * M_WIDTH + lo:idx * M_WIDTH + lo + C_ROWS, :]
                out_ref[0, lo:lo + C_ROWS, toks] = lax.dot_general(
                    rows, hb, NT_DIMS, preferred_element_type=f32).astype(bf16)


def _in_ch(x, mod, gpre, wt, tm):
    bsz, n, _ = x.shape
    ch = jax.ShapeDtypeStruct((bsz, M_WIDTH, n), bf16)
    ch_spec = pl.BlockSpec((1, M_WIDTH, tm), lambda b, i: (b, 0, i))
    return pl.pallas_call(
        _in_ch_kernel,
        grid=(bsz, n // tm),
        in_specs=[pl.BlockSpec((1, tm, D_MODEL), lambda b, i: (b, i, 0)),
                  pl.BlockSpec((1, 1, 3 * D_MODEL), lambda b, i: (b % mod.shape[0], 0, 0)),
                  pl.BlockSpec((1, D_MODEL), lambda b, i: (0, 0)),
                  pl.BlockSpec(wt.shape, lambda b, i: (0, 0))],
        out_specs=[ch_spec, ch_spec, ch_spec],
        out_shape=[ch, ch, ch],
        compiler_params=_params("parallel", "parallel"),
        name="in_proj_ab_ch",
    )(x, mod, gpre, wt)


def _log_sigmoid(a):
    return jnp.minimum(a, 0.0) - jnp.log1p(jnp.exp(-jnp.abs(a)))


def _gate_rows(g_ref, r_scr):
    g = g_ref[0]
    n = g.shape[0]
    sub = lax.broadcasted_iota(jnp.int32, (N_GATE, GATE_PAD), 0)
    lane = lax.broadcasted_iota(jnp.int32, (N_GATE, GATE_PAD), 1)
    pick = jnp.where(lane == (sub % 4) * M_HEADS + sub // 4, 1.0, 0.0).astype(bf16)
    rows = jnp.zeros((N_GATE, n), f32)
    rest = g
    for _ in range(3):
        piece = rest.astype(bf16)
        rest = rest - piece.astype(f32)
        rows = rows + lax.dot_general(pick, piece, NT_DIMS, preferred_element_type=f32)
    logf = _log_sigmoid(rows)
    pos = lax.broadcasted_iota(jnp.int32, (N_GATE, n), 1) % M_CHUNK
    pre, suf = logf, logf
    shift = 1
    while shift < M_CHUNK:
        pre = pre + jnp.where(pos >= shift, pltpu.roll(pre, shift, 1), 0.0)
        suf = suf + jnp.where(pos < M_CHUNK - shift, pltpu.roll(suf, n - shift, 1), 0.0)
        shift *= 2
    kind = lax.broadcasted_iota(jnp.int32, (N_GATE, n), 0) % 4
    final = jnp.where(kind == 1, pre, jnp.where(kind == 3, suf, rows))
    for h in range(M_HEADS):
        r_scr[h, 0:4, :] = final[4 * h:4 * h + 4]


STATE_ROWS = M_HEAD_DIM + BF16_ROWS


def _mlstm_chunk(seg, consts, state, start, direction, first):
    q_ref, k_ref, vt_ref, ot_ref, zt_ref, r_scr, acc, y_ref = seg
    mn_ref, head = consts
    ct_scr, m_scr = state
    if not isinstance(start, int):
        start = pl.multiple_of(start, M_CHUNK)
    span = pl.ds(start, M_CHUNK)
    dk = M_HEAD_DIM

    q = q_ref[0, span, :]
    k = k_ref[0, span, :]
    vt = vt_ref[0, :, span]

    li = r_scr[head, 2 * direction:2 * direction + 1, span]
    bcum = r_scr[head, 2 * direction + 1:2 * direction + 2, span]
    si = lax.broadcasted_iota(jnp.int32, (M_CHUNK, M_CHUNK), 0)
    ti = lax.broadcasted_iota(jnp.int32, (M_CHUNK, M_CHUNK), 1)
    causal = (si <= ti) if direction == 0 else (si >= ti)
    b_last = bcum[:, M_CHUNK - 1:M_CHUNK] if direction == 0 else bcum[:, 0:1]

    m_prev = m_scr[direction]
    ct = ct_scr[direction]
    n_mem = ct[dk:dk + 1, :]

    key_side = jnp.transpose(jnp.broadcast_to(li - bcum, (M_CHUNK, M_CHUNK)))
    dmat = jnp.where(causal, bcum + key_side, -jnp.inf)
    inter = bcum + m_prev
    m_t = jnp.maximum(inter, jnp.max(dmat, axis=0, keepdims=True))
    w_inter = jnp.exp(inter - m_t)
    s = lax.dot_general(k, q, NT_DIMS, preferred_element_type=f32) * jnp.exp(dmat - m_t)

    tail = lax.broadcasted_iota(jnp.int32, (BF16_ROWS, dk), 0)
    n_hi = n_mem.astype(bf16).astype(f32)
    extra = jnp.where(tail == 0, n_hi, jnp.where(tail == 1, n_mem - n_hi, 0.0))
    lhs = jnp.concatenate([ct[:dk].astype(bf16), extra.astype(bf16)], axis=0)
    from_state = lax.dot_general(lhs, q, NT_DIMS, preferred_element_type=f32)
    num = w_inter * from_state[:dk] + jnp.dot(vt, s.astype(bf16), preferred_element_type=f32)
    den = (w_inter * (from_state[dk:dk + 1] + from_state[dk + 1:dk + 2])
           + jnp.sum(s, axis=0, keepdims=True))
    h = num * (1.0 / jnp.maximum(jnp.abs(den), jnp.exp(-m_t)))

    d_last = b_last - bcum + li
    m_new = jnp.maximum(b_last + m_prev, jnp.max(d_last, axis=-1, keepdims=True))
    w_row = jnp.exp(d_last - m_new)
    decay = jnp.exp(b_last + m_prev - m_new)
    w_hi = w_row.astype(bf16).astype(f32)
    w_extra = jnp.where(tail[:, :M_CHUNK] == 0, w_hi, jnp.where(tail[:, :M_CHUNK] == 1, w_row - w_hi, 0.0))
    va = jnp.concatenate([(vt.astype(f32) * w_row).astype(bf16), w_extra.astype(bf16)], axis=0)
    update = jnp.dot(va, k, preferred_element_type=f32)
    ct_scr[direction, 0:dk, :] = decay * ct[:dk] + update[:dk]
    n_new = decay * n_mem + update[dk:dk + 1] + update[dk + 1:dk + 2]
    ct_scr[direction, dk:, :] = jnp.where(tail == 0, n_new, 0.0)
    m_scr[direction] = m_new

    if first:
        acc[:, span] = h
    else:
        hg = (acc[:, span] + h) * jax.nn.sigmoid(ot_ref[0, :, span].astype(f32))
        hn = hg * lax.rsqrt(jnp.mean(hg * hg, axis=0, keepdims=True) + EPS) * mn_ref[...]
        y_ref[0, :, span] = (hn * _silu(zt_ref[0, :, span].astype(f32))).astype(bf16)


def _mlstm_kernel(qx_ref, kx_ref, vx_ref, ox_ref, zx_ref, gx_ref,
                  qc_ref, kc_ref, vc_ref, oc_ref, zc_ref, gc_ref,
                  mn_ref, yx_ref, yc_ref,
                  rx_scr, rc_scr, accx, accc, ct_scr, m_scr):
    head = pl.program_id(1)
    nx, nc = qx_ref.shape[1], qc_ref.shape[1]

    @pl.when(head == 0)
    def _():
        _gate_rows(gx_ref, rx_scr)
        _gate_rows(gc_ref, rc_scr)

    ct_scr[...] = jnp.zeros_like(ct_scr)
    m_scr[...] = jnp.zeros_like(m_scr)

    consts = (mn_ref, head)
    state = (ct_scr, m_scr)
    ctx = (qc_ref, kc_ref, vc_ref, oc_ref, zc_ref, rc_scr, accc, yc_ref)
    lat = (qx_ref, kx_ref, vx_ref, ox_ref, zx_ref, rx_scr, accx, yx_ref)
    step = functools.partial(_mlstm_chunk, consts=consts, state=state)

    ncc = nc // M_CHUNK
    assert ncc == 2
    step(ctx, start=0, direction=0, first=True)
    step(ctx, start=M_CHUNK, direction=1, first=True)
    step(ctx, start=M_CHUNK, direction=0, first=False)
    step(ctx, start=0, direction=1, first=False)

    nxc = nx // M_CHUNK

    def meet(first):
        def body(i, carry):
            step(lat, start=i * M_CHUNK, direction=0, first=first)
            step(lat, start=(nxc - 1 - i) * M_CHUNK, direction=1, first=first)
            return carry
        return body

    lax.fori_loop(0, nxc // 2, meet(True), 0, unroll=8)
    lax.fori_loop(nxc // 2, nxc, meet(False), 0, unroll=4)


def _mlstm(lat, ctx, mnorm_tab):
    bsz, nx, _ = lat[0].shape
    nc = ctx[0].shape[1]
    hd = M_HEAD_DIM

    def specs(n):
        tok = pl.BlockSpec((1, n, hd), lambda b, h: (b, 0, h))
        ch = pl.BlockSpec((1, hd, n), lambda b, h: (b, h, 0))
        return [tok, tok, ch, ch, ch, pl.BlockSpec((1, n, GATE_PAD), lambda b, h: (b, 0, 0))]

    return pl.pallas_call(
        _mlstm_kernel,
        grid=(bsz, M_HEADS),
        in_specs=specs(nx) + specs(nc) + [pl.BlockSpec((hd, 128), lambda b, h: (h, 0))],
        out_specs=[pl.BlockSpec((1, hd, nx), lambda b, h: (b, h, 0)),
                   pl.BlockSpec((1, hd, nc), lambda b, h: (b, h, 0))],
        out_shape=[jax.ShapeDtypeStruct((bsz, M_WIDTH, nx), bf16),
                   jax.ShapeDtypeStruct((bsz, M_WIDTH, nc), bf16)],
        scratch_shapes=[pltpu.VMEM((M_HEADS, 8, nx), f32), pltpu.VMEM((M_HEADS, 8, nc), f32),
                        pltpu.VMEM((hd, nx), f32), pltpu.VMEM((hd, nc), f32),
                        pltpu.VMEM((2, STATE_ROWS, hd), f32), pltpu.VMEM((2, 1, 1), f32)],
        compiler_params=_params("parallel", "arbitrary"),
        name="mlstm",
    )(*lat, *ctx, mnorm_tab)


P_HALO = 16


def _out_ab_kernel(ym_ref, xp_ref, xpp_ref, xpn_ref, zp_ref, x_ref, mod_ref, gpost_ref,
                   wo_ref, pw_ref, ps_ref, o_ref, ext_scr):
    i, nt = pl.program_id(1), pl.num_programs(1)
    tm = x_ref.shape[1]
    ext_scr[0:P_HALO, :] = jnp.where(i > 0, xpp_ref[0].astype(f32), 0.0)
    ext_scr[P_HALO:P_HALO + tm, :] = xp_ref[0].astype(f32)
    ext_scr[P_HALO + tm:, :] = jnp.where(i < nt - 1, xpn_ref[0].astype(f32), 0.0)

    n = nt * tm
    ext_rows = tm + 2 * P_HALO
    t = i * tm + lax.broadcasted_iota(jnp.int32, (tm, 1), 0)
    y = jnp.dot(ym_ref[0].T, wo_ref[0:M_WIDTH, :], preferred_element_type=f32)
    for g, win in enumerate(P_WINDOWS):
        half = win // 2
        cols = slice(g * P_GROUP_DIM, (g + 1) * P_GROUP_DIM)
        e = ext_scr[:, cols]
        run, width = e, 1
        while width < win:
            run = run + pltpu.roll(run, ext_rows - width, 0)
            width *= 2
        total = pltpu.roll(run, half, 0)[P_HALO:P_HALO + tm]
        count = (jnp.minimum(t + half, n) - jnp.maximum(t - half, 0)).astype(f32)
        pooled = total * (1.0 / count) - e[P_HALO:P_HALO + tm]
        mixed = jnp.dot(pooled.astype(bf16), pw_ref[g], preferred_element_type=f32) * ps_ref[:, cols]
        yp = (mixed * _silu(zp_ref[0, :, cols].astype(f32))).astype(bf16)
        y = y + jnp.dot(yp, wo_ref[M_WIDTH + g * P_GROUP_DIM:M_WIDTH + (g + 1) * P_GROUP_DIM, :],
                        preferred_element_type=f32)
    o_ref[0] = _gated_residual(x_ref[0], y, gpost_ref[...], mod_ref[0])


def _out_ab(ymt, xp, zp, x, mod, gpost, wo, pw, ps, tm):
    bsz, n, _ = x.shape
    nh = n // P_HALO
    th = tm // P_HALO
    return pl.pallas_call(
        _out_ab_kernel,
        grid=(bsz, n // tm),
        in_specs=[pl.BlockSpec((1, M_WIDTH, tm), lambda b, i: (b, 0, i)),
                  pl.BlockSpec((1, tm, P_WIDTH), lambda b, i: (b, i, 0)),
                  pl.BlockSpec((1, P_HALO, P_WIDTH), lambda b, i: (b, jnp.maximum(i * th - 1, 0), 0)),
                  pl.BlockSpec((1, P_HALO, P_WIDTH), lambda b, i: (b, jnp.minimum((i + 1) * th, nh - 1), 0)),
                  pl.BlockSpec((1, tm, P_WIDTH), lambda b, i: (b, i, 0)),
                  pl.BlockSpec((1, tm, D_MODEL), lambda b, i: (b, i, 0)),
                  pl.BlockSpec((1, 1, 3 * D_MODEL), lambda b, i: (b % mod.shape[0], 0, 0)),
                  pl.BlockSpec((1, D_MODEL), lambda b, i: (0, 0)),
                  pl.BlockSpec((M_WIDTH + P_WIDTH, D_MODEL), lambda b, i: (0, 0)),
                  pl.BlockSpec((len(P_WINDOWS), P_GROUP_DIM, P_GROUP_DIM), lambda b, i: (0, 0, 0)),
                  pl.BlockSpec((1, P_WIDTH), lambda b, i: (0, 0))],
        out_specs=pl.BlockSpec((1, tm, D_MODEL), lambda b, i: (b, i, 0)),
        out_shape=jax.ShapeDtypeStruct((bsz, n, D_MODEL), f32),
        scratch_shapes=[pltpu.VMEM((tm + 2 * P_HALO, P_WIDTH), f32)],
        compiler_params=_params("parallel", "parallel"),
        name="out_proj_ab",
    )(ymt, xp, xp, xp, zp, x, mod, gpost, wo, pw, ps)


def _rope(a, cos, sin):
    lane = lax.broadcasted_iota(jnp.int32, a.shape, 1)
    half = A_HEAD_DIM // 2
    partner = jnp.where(lane % A_HEAD_DIM < half,
                        pltpu.roll(a, a.shape[1] - half, 1), pltpu.roll(a, half, 1))
    return a * cos + partner * sin


LOG2E = 1.4426950408889634


def _in_c_kernel(x_ref, mod_ref, gpre_ref, wt_ref, wk_ref, cost_ref, sint_ref, cos_ref, sin_ref,
                 qt_ref, k_ref, vt_ref, zt_ref, *, rope):
    half = A_HEAD_DIM // 2
    q_scale = (A_HEAD_DIM ** -0.5) * LOG2E
    for t0 in range(0, x_ref.shape[1], SUB_ROWS):
        toks = slice(t0, t0 + SUB_ROWS)
        hb = _modulated_norm(x_ref[0, toks, :], gpre_ref[...], mod_ref[0]).astype(bf16)

        def channel_major(lo, rows):
            return lax.dot_general(wt_ref[lo:lo + rows, :], hb, NT_DIMS, preferred_element_type=f32)

        for lo in range(0, A_WIDTH, C_ROWS):
            a = channel_major(lo, C_ROWS)
            for h in range(C_ROWS // A_HEAD_DIM):
                x1 = a[h * A_HEAD_DIM:h * A_HEAD_DIM + half]
                x2 = a[h * A_HEAD_DIM + half:(h + 1) * A_HEAD_DIM]
                if rope:
                    c, s = cost_ref[:, toks], sint_ref[:, toks]
                    x1, x2 = x1 * c - x2 * s, x1 * s + x2 * c
                r0 = lo + h * A_HEAD_DIM
                qt_ref[0, r0:r0 + half, toks] = (x1 * q_scale).astype(bf16)
                qt_ref[0, r0 + half:r0 + A_HEAD_DIM, toks] = (x2 * q_scale).astype(bf16)
        vt_ref[0, :, toks] = channel_major(A_WIDTH, A_KV_WIDTH).astype(bf16)
        for lo in range(0, A_WIDTH, C_ROWS):
            zt_ref[0, lo:lo + C_ROWS, toks] = channel_major(A_WIDTH + A_KV_WIDTH + lo, C_ROWS).astype(bf16)
        for j in range(A_KV_WIDTH // 128):
            kk = jnp.dot(hb, wk_ref[:, j * 128:(j + 1) * 128], preferred_element_type=f32)
            if rope:
                kk = _rope(kk, cos_ref[toks, :], sin_ref[toks, :])
            k_ref[0, toks, j * 128:(j + 1) * 128] = kk.astype(bf16)


def _in_c(x, mod, gpre, wt, wk, cost, sint, cos, sin, tm, rope):
    bsz, n, _ = x.shape
    half = A_HEAD_DIM // 2
    return pl.pallas_call(
        functools.partial(_in_c_kernel, rope=rope),
        grid=(bsz, n // tm),
        in_specs=[pl.BlockSpec((1, tm, D_MODEL), lambda b, i: (b, i, 0)),
                  pl.BlockSpec((1, 1, 3 * D_MODEL), lambda b, i: (b % mod.shape[0], 0, 0)),
                  pl.BlockSpec((1, D_MODEL), lambda b, i: (0, 0)),
                  pl.BlockSpec(wt.shape, lambda b, i: (0, 0)),
                  pl.BlockSpec(wk.shape, lambda b, i: (0, 0)),
                  pl.BlockSpec((half, tm), lambda b, i: (0, i)),
                  pl.BlockSpec((half, tm), lambda b, i: (0, i)),
                  pl.BlockSpec((tm, 128), lambda b, i: (i, 0)),
                  pl.BlockSpec((tm, 128), lambda b, i: (i, 0))],
        out_specs=[pl.BlockSpec((1, A_WIDTH, tm), lambda b, i: (b, 0, i)),
                   pl.BlockSpec((1, tm, A_KV_WIDTH), lambda b, i: (b, i, 0)),
                   pl.BlockSpec((1, A_KV_WIDTH, tm), lambda b, i: (b, 0, i)),
                   pl.BlockSpec((1, A_WIDTH, tm), lambda b, i: (b, 0, i))],
        out_shape=[jax.ShapeDtypeStruct((bsz, A_WIDTH, n), bf16),
                   jax.ShapeDtypeStruct((bsz, n, A_KV_WIDTH), bf16),
                   jax.ShapeDtypeStruct((bsz, A_KV_WIDTH, n), bf16),
                   jax.ShapeDtypeStruct((bsz, A_WIDTH, n), bf16)],
        compiler_params=_params("parallel", "parallel"),
        name="in_proj_c",
    )(x, mod, gpre, wt, wk, cost, sint, cos, sin)


def _attn_kernel(sink_ref, qt_ref, k_ref, vt_ref, kc_ref, vct_ref, zt_ref, ogt_ref, s_scr, *, banded):
    nc = kc_ref.shape[1]
    blocks = qt_ref.shape[2] // A_BLOCK
    band_start, bias = [], []
    if banded:
        n = k_ref.shape[1]
        for qb in range(blocks):
            start = (pl.program_id(1) * blocks + qb) * A_BLOCK
            bs = pl.multiple_of(jnp.clip(start - A_WINDOW, 0, n - A_BAND), A_BLOCK)
            kj = bs + lax.broadcasted_iota(jnp.int32, (A_BAND, A_BLOCK), 0)
            qi = start + lax.broadcasted_iota(jnp.int32, (A_BAND, A_BLOCK), 1)
            mask = jnp.where(jnp.abs(qi - kj) <= A_WINDOW, 0.0, -jnp.inf)
            band_start.append(bs)
            bias.append(jnp.concatenate([mask] * A_GROUP, axis=1))

    def scores(u):
        qb, g = divmod(u, A_KV_HEADS)
        lanes = slice(qb * A_BLOCK, (qb + 1) * A_BLOCK)
        heads = range(g * A_GROUP, (g + 1) * A_GROUP)
        pair = slice(128 * (g // 2), 128 * (g // 2) + 128)
        qg = jnp.concatenate([qt_ref[0, h * A_HEAD_DIM:(h + 1) * A_HEAD_DIM, lanes] for h in heads], axis=1)
        zero = jnp.zeros_like(qg)
        qpad = jnp.concatenate([qg, zero] if g % 2 == 0 else [zero, qg], axis=0)
        slot = u % s_scr.shape[0]
        s_scr[slot, 0:nc, :] = jnp.dot(kc_ref[0, :, pair], qpad, preferred_element_type=f32)
        if banded:
            s_scr[slot, nc:, :] = jnp.dot(k_ref[0, pl.ds(band_start[qb], A_BAND), pair], qpad,
                                       preferred_element_type=f32) + bias[qb]

    def softmax(u):
        g = u % A_KV_HEADS
        heads = range(g * A_GROUP, (g + 1) * A_GROUP)
        sk = jnp.concatenate([jnp.full((1, A_BLOCK), sink_ref[h] * LOG2E, f32) for h in heads], axis=1)
        s = s_scr[u % s_scr.shape[0]]
        m = jnp.maximum(jnp.max(s, axis=0, keepdims=True), sk)
        p = jnp.exp2(s - m)
        inv = 1.0 / (jnp.sum(p, axis=0, keepdims=True) + jnp.exp2(sk - m))
        return p.astype(bf16), inv

    def output(u, pb, inv):
        qb, g = divmod(u, A_KV_HEADS)
        lanes = slice(qb * A_BLOCK, (qb + 1) * A_BLOCK)
        heads = range(g * A_GROUP, (g + 1) * A_GROUP)
        kvrows = slice(g * A_HEAD_DIM, (g + 1) * A_HEAD_DIM)
        ot = jnp.dot(vct_ref[0, kvrows, :], pb[0:nc], preferred_element_type=f32)
        if banded:
            ot = ot + jnp.dot(vt_ref[0, kvrows, pl.ds(band_start[qb], A_BAND)], pb[nc:],
                              preferred_element_type=f32)
        ot = ot * inv
        for a, h in enumerate(heads):
            rows = slice(h * A_HEAD_DIM, (h + 1) * A_HEAD_DIM)
            gate = _silu(zt_ref[0, rows, lanes].astype(f32))
            ogt_ref[0, rows, lanes] = (ot[:, a * A_BLOCK:(a + 1) * A_BLOCK] * gate).astype(bf16)

    units = blocks * A_KV_HEADS
    scores(0)
    scores(1)
    pending = None
    for u in range(units):
        sm = softmax(u)
        if u + 2 < units:
            scores(u + 2)
        if pending is not None:
            output(u - 1, *pending)
        pending = sm
    output(units - 1, *pending)


def _attn(sink, qt, k, vt, kc, vct, zt, banded):
    bsz, _, nq = qt.shape
    nk, nc = k.shape[1], kc.shape[1]
    nkeys = nc + (A_BAND if banded else 0)
    blocks = min(A_QUERY_BLOCKS, nq // A_BLOCK)
    tq = blocks * A_BLOCK
    return pl.pallas_call(
        functools.partial(_attn_kernel, banded=banded),
        grid=(bsz, nq // tq),
        in_specs=[pl.BlockSpec(memory_space=pltpu.SMEM),
                  pl.BlockSpec((1, A_WIDTH, tq), lambda b, i: (b, 0, i)),
                  pl.BlockSpec((1, nk, A_KV_WIDTH), lambda b, i: (b, 0, 0)),
                  pl.BlockSpec((1, A_KV_WIDTH, nk), lambda b, i: (b, 0, 0)),
                  pl.BlockSpec((1, nc, A_KV_WIDTH), lambda b, i: (b, 0, 0)),
                  pl.BlockSpec((1, A_KV_WIDTH, nc), lambda b, i: (b, 0, 0)),
                  pl.BlockSpec((1, A_WIDTH, tq), lambda b, i: (b, 0, i))],
        out_specs=pl.BlockSpec((1, A_WIDTH, tq), lambda b, i: (b, 0, i)),
        out_shape=jax.ShapeDtypeStruct((bsz, A_WIDTH, nq), bf16),
        scratch_shapes=[pltpu.VMEM((4, nkeys, A_GROUP * A_BLOCK), f32)],
        compiler_params=_params("parallel", "parallel"),
        name="attention_banded" if banded else "attention_context",
    )(sink, qt, k, vt, kc, vct, zt)


def _out_c_kernel(at_ref, x_ref, mod_ref, gpost_ref, wo_ref, o_ref):
    y = jnp.dot(at_ref[0].T, wo_ref[...], preferred_element_type=f32)
    o_ref[0] = _gated_residual(x_ref[0], y, gpost_ref[...], mod_ref[0])


def _out_c(at, x, mod, gpost, wo, tm):
    bsz, n, _ = x.shape
    return pl.pallas_call(
        _out_c_kernel,
        grid=(bsz, n // tm),
        in_specs=[pl.BlockSpec((1, A_WIDTH, tm), lambda b, i: (b, 0, i)),
                  pl.BlockSpec((1, tm, D_MODEL), lambda b, i: (b, i, 0)),
                  pl.BlockSpec((1, 1, 3 * D_MODEL), lambda b, i: (b % mod.shape[0], 0, 0)),
                  pl.BlockSpec((1, D_MODEL), lambda b, i: (0, 0)),
                  pl.BlockSpec((A_WIDTH, D_MODEL), lambda b, i: (0, 0))],
        out_specs=pl.BlockSpec((1, tm, D_MODEL), lambda b, i: (b, i, 0)),
        out_shape=jax.ShapeDtypeStruct((bsz, n, D_MODEL), f32),
        compiler_params=_params("parallel", "parallel"),
        name="out_proj_c",
    )(at, x, mod, gpost, wo)


def _rope_tables(n):
    rows = n // GRID_W
    row = np.repeat(np.arange(rows), GRID_W).astype(np.float64)
    col = np.tile(np.arange(GRID_W), rows).astype(np.float64)
    n_freq = A_HEAD_DIM // 4
    inv = ROPE_BASE ** (-np.arange(n_freq, dtype=np.float64) / n_freq)
    ang = np.concatenate([row[:, None] * inv, col[:, None] * inv], -1)
    cos, sin = np.cos(ang).astype(np.float32), np.sin(ang).astype(np.float32)
    cos_t = np.tile(np.concatenate([cos, cos], -1), (1, 128 // A_HEAD_DIM))
    sin_t = np.tile(np.concatenate([-sin, sin], -1), (1, 128 // A_HEAD_DIM))
    return tuple(jnp.asarray(np.ascontiguousarray(a)) for a in (cos.T, sin.T, cos_t, sin_t))


def _split_pairs_columns(w, heads):
    perm = np.concatenate([np.arange(0, A_HEAD_DIM, 2), np.arange(1, A_HEAD_DIM, 2)])
    idx = (np.arange(heads)[:, None] * A_HEAD_DIM + perm[None, :]).reshape(-1)
    return w[:, idx]


def kernel(x, c, ctx, c_ctx, w_mod, b_mod, g_pre, g_post, ab_w_in, ab_b_gate, ab_conv, ab_mnorm,
           ab_pool_w, ab_pool_scale, ab_w_out, c_w_in, c_sink, c_w_out):
    bsz, n, _ = x.shape
    n_ctx = ctx.shape[1]
    assert bsz < 16
    cs = jnp.zeros((16, D_MODEL), f32).at[:bsz].set(c).at[bsz].set(c_ctx)
    mod_all = _modulation(cs, w_mod, b_mod)
    cost, sint, cos, sin = _rope_tables(n)
    tm_x = 1024

    for l in range(DEPTH):
        j = l // 2
        last = l == DEPTH - 1
        mod_x = mod_all[l, :bsz].reshape(bsz, 1, 3 * D_MODEL)
        mod_c = mod_all[l, bsz:bsz + 1].reshape(1, 1, 3 * D_MODEL)
        gpre = g_pre[l].reshape(1, D_MODEL)
        gpost = g_post[l].reshape(1, D_MODEL)
        if l % 2 == 0:
            w = ab_w_in[j].astype(bf16)
            w_tok = jnp.concatenate([w[:, :2 * M_WIDTH], w[:, 5 * M_WIDTH:GATE_OFF]], axis=1)
            wt_ch = w[:, 2 * M_WIDTH:5 * M_WIDTH].T
            wg = jnp.pad(w[:, GATE_OFF:], ((0, 0), (0, GATE_PAD - N_GATE)))
            bg = jnp.pad(ab_b_gate[j], (0, GATE_PAD - N_GATE)).reshape(1, GATE_PAD)
            k_scale = jnp.concatenate([jnp.ones((M_WIDTH,), f32), jnp.full((M_WIDTH,), M_HEAD_DIM ** -0.5, f32)])
            cw = ab_conv[j] * k_scale
            mnorm_tab = jnp.broadcast_to(ab_mnorm[j][:, None], (M_WIDTH, 128))

            def project(a, mod_a, tm_tok, tm_ch):
                q, k, xp, zp, g = _in_tok(a, mod_a, gpre, w_tok, wg, bg, cw, tm_tok)
                vt, ot, zt = _in_ch(a, mod_a, gpre, wt_ch, tm_ch)
                return (q, k, vt, ot, zt, g), xp, zp

            lat, xpx, zpx = project(x, mod_x, 1024, 512)
            con, xpc, zpc = project(ctx, mod_c, n_ctx, n_ctx)
            ymx, ymc = _mlstm(lat, con, mnorm_tab)
            wo = ab_w_out[j].astype(bf16)
            pw = ab_pool_w[j].astype(bf16)
            ps = ab_pool_scale[j].reshape(1, P_WIDTH)
            x = _out_ab(ymx, xpx, zpx, x, mod_x, gpost, wo, pw, ps, 512)
            if not last:
                ctx = _out_ab(ymc, xpc, zpc, ctx, mod_c, gpost, wo, pw, ps, n_ctx)
        else:
            w = c_w_in[j].astype(bf16)
            wt = jnp.concatenate([_split_pairs_columns(w[:, :A_WIDTH], A_HEADS),
                                  w[:, A_WIDTH + A_KV_WIDTH:]], axis=1).T
            wk = _split_pairs_columns(w[:, A_WIDTH:A_WIDTH + A_KV_WIDTH], A_KV_HEADS)
            qx, kx, vx, zx = _in_c(x, mod_x, gpre, wt, wk, cost, sint, cos, sin, 512, True)
            qc, kc, vc, zc = _in_c(ctx, mod_c, gpre, wt, wk, cost[:, :n_ctx], sint[:, :n_ctx],
                                   cos[:n_ctx], sin[:n_ctx], n_ctx, False)
            wo = c_w_out[j].astype(bf16)
            ax = _attn(c_sink[j], qx, kx, vx, kc, vc, zx, True)
            x = _out_c(ax, x, mod_x, gpost, wo, tm_x)
            if not last:
                ac = _attn(c_sink[j], qc, kc, vc, kc, vc, zc, False)
                ctx = _out_c(ac, ctx, mod_c, gpost, wo, n_ctx)
    return x
```

```python
import functools

import jax
import jax.numpy as jnp
import numpy as np
from jax import lax
from jax.experimental import pallas as pl
from jax.experimental.pallas import tpu as pltpu

f32 = jnp.float32
bf16 = jnp.bfloat16

D_MODEL = 1024
DEPTH = 4
EPS = 1e-6
GRID_W = 64

M_HEADS = 4
M_HEAD_DIM = 256
M_WIDTH = 1024
M_CHUNK = 128
P_WINDOWS = (2, 4, 8, 16)
P_GROUP_DIM = 256
P_WIDTH = 1024
N_GATE = 16
GATE_OFF = 5 * M_WIDTH + 2 * P_WIDTH
GATE_PAD = 128

A_HEADS = 16
A_KV_HEADS = 4
A_GROUP = 4
A_HEAD_DIM = 64
A_WIDTH = 1024
A_KV_WIDTH = 256
A_WINDOW = 128
A_BLOCK = 128
A_BAND = A_BLOCK + 2 * A_WINDOW
A_QUERY_BLOCKS = 4
ROPE_BASE = 10000.0

BF16_ROWS = 16
VMEM_LIMIT = 52 * 1024 * 1024


def _params(*sem):
    return pltpu.CompilerParams(dimension_semantics=sem, vmem_limit_bytes=VMEM_LIMIT)


def _silu(a):
    return a * jax.nn.sigmoid(a)


def _modulated_norm(x, gpre, mod):
    xn = x * lax.rsqrt(jnp.mean(x * x, -1, keepdims=True) + EPS) * gpre
    return xn * (1.0 + mod[:, D_MODEL:2 * D_MODEL]) + mod[:, :D_MODEL]


def _gated_residual(x, y, gpost, mod):
    r = y * lax.rsqrt(jnp.mean(y * y, -1, keepdims=True) + EPS) * gpost
    return x + mod[:, 2 * D_MODEL:] * r


def _mod_kernel(cs_ref, w_ref, b_ref, o_ref):
    a = _silu(cs_ref[...])
    o_ref[0] = jnp.dot(a, w_ref[0], preferred_element_type=f32,
                       precision=lax.Precision.HIGHEST) + b_ref[0]


def _modulation(cs, w_mod, b_mod):
    tn = 1024
    return pl.pallas_call(
        _mod_kernel,
        grid=(DEPTH, 3 * D_MODEL // tn),
        in_specs=[pl.BlockSpec((16, D_MODEL), lambda l, j: (0, 0)),
                  pl.BlockSpec((1, D_MODEL, tn), lambda l, j: (l, 0, j)),
                  pl.BlockSpec((1, 1, tn), lambda l, j: (l, 0, j))],
        out_specs=pl.BlockSpec((1, 16, tn), lambda l, j: (l, 0, j)),
        out_shape=jax.ShapeDtypeStruct((DEPTH, 16, 3 * D_MODEL), f32),
        compiler_params=_params("parallel", "parallel"),
        name="modulation",
    )(cs, w_mod, b_mod.reshape(DEPTH, 1, 3 * D_MODEL))


NT_DIMS = (((1,), (1,)), ((), ()))
C_ROWS = 512
HALO = 16
CONV_COLS = 256
SUB_ROWS = 256


def _in_tok_kernel(x_ref, xprev_ref, xnext_ref, mod_ref, gpre_ref, w_ref, wg_ref, bg_ref, cw_ref,
                   q_ref, k_ref, xp_ref, zp_ref, g_ref, h_scr):
    i, nt, j = pl.program_id(1), pl.num_programs(1), pl.program_id(2)
    tm = x_ref.shape[1]

    @pl.when(j == 0)
    def _():
        def normed(a):
            return _modulated_norm(a, gpre_ref[...], mod_ref[0]).astype(bf16)

        h_scr[0:HALO, :] = jnp.where(i > 0, normed(xprev_ref[0]), 0.0).astype(bf16)
        h_scr[HALO:HALO + tm, :] = normed(x_ref[0])
        h_scr[HALO + tm:, :] = jnp.where(i < nt - 1, normed(xnext_ref[0]), 0.0).astype(bf16)
        g_ref[0] = jnp.dot(h_scr[HALO:HALO + tm, :], wg_ref[...], preferred_element_type=f32) + bg_ref[...]

        ext = tm + 2 * HALO
        for half, out_ref in enumerate((q_ref, k_ref)):
            for lo in range(0, M_WIDTH, CONV_COLS):
                src = slice(half * M_WIDTH + lo, half * M_WIDTH + lo + CONV_COLS)
                r = jnp.dot(h_scr[...], w_ref[:, src], preferred_element_type=f32)
                conv = (cw_ref[0:1, src] * pltpu.roll(r, 1, 0) + cw_ref[1:2, src] * r
                        + cw_ref[2:3, src] * pltpu.roll(r, ext - 1, 0))
                out_ref[0, :, lo:lo + CONV_COLS] = conv[HALO:HALO + tm].astype(bf16)

    @pl.when(j == 1)
    def _():
        for half, out_ref in enumerate((xp_ref, zp_ref)):
            for lo in range(0, P_WIDTH, CONV_COLS):
                src = slice(half * P_WIDTH + lo, half * P_WIDTH + lo + CONV_COLS)
                out_ref[0, :, lo:lo + CONV_COLS] = jnp.dot(
                    h_scr[HALO:HALO + tm, :], w_ref[:, src], preferred_element_type=f32).astype(bf16)


def _in_tok(x, mod, gpre, w, wg, bg, cw, tm):
    bsz, n, _ = x.shape
    th, nh = tm // HALO, n // HALO
    tok = jax.ShapeDtypeStruct((bsz, n, M_WIDTH), bf16)
    tok_spec = pl.BlockSpec((1, tm, M_WIDTH), lambda b, i, j: (b, i, 0))
    return pl.pallas_call(
        _in_tok_kernel,
        grid=(bsz, n // tm, 2),
        in_specs=[pl.BlockSpec((1, tm, D_MODEL), lambda b, i, j: (b, i, 0)),
                  pl.BlockSpec((1, HALO, D_MODEL), lambda b, i, j: (b, jnp.maximum(i * th - 1, 0), 0)),
                  pl.BlockSpec((1, HALO, D_MODEL), lambda b, i, j: (b, jnp.minimum((i + 1) * th, nh - 1), 0)),
                  pl.BlockSpec((1, 1, 3 * D_MODEL), lambda b, i, j: (b % mod.shape[0], 0, 0)),
                  pl.BlockSpec((1, D_MODEL), lambda b, i, j: (0, 0)),
                  pl.BlockSpec((D_MODEL, 2 * M_WIDTH), lambda b, i, j: (0, j)),
                  pl.BlockSpec((D_MODEL, GATE_PAD), lambda b, i, j: (0, 0)),
                  pl.BlockSpec((1, GATE_PAD), lambda b, i, j: (0, 0)),
                  pl.BlockSpec((3, 2 * M_WIDTH), lambda b, i, j: (0, 0))],
        out_specs=[tok_spec, tok_spec, tok_spec, tok_spec,
                   pl.BlockSpec((1, tm, GATE_PAD), lambda b, i, j: (b, i, 0))],
        out_shape=[tok, tok, tok, tok, jax.ShapeDtypeStruct((bsz, n, GATE_PAD), f32)],
        scratch_shapes=[pltpu.VMEM((tm + 2 * HALO, D_MODEL), bf16)],
        compiler_params=_params("parallel", "parallel", "arbitrary"),
        name="in_proj_ab_tok",
    )(x, x, x, mod, gpre, w, wg, bg, cw)


def _in_ch_kernel(x_ref, mod_ref, gpre_ref, wt_ref, vt_ref, ot_ref, zt_ref):
    for t0 in range(0, x_ref.shape[1], SUB_ROWS):
        toks = slice(t0, t0 + SUB_ROWS)
        hb = _modulated_norm(x_ref[0, toks, :], gpre_ref[...], mod_ref[0]).astype(bf16)
        for idx, out_ref in enumerate((vt_ref, ot_ref, zt_ref)):
            for lo in range(0, M_WIDTH, C_ROWS):
                rows = wt_ref[idx * M_WIDTH + lo:idx * M_WIDTH + lo + C_ROWS, :]
                out_ref[0, lo:lo + C_ROWS, toks] = lax.dot_general(
                    rows, hb, NT_DIMS, preferred_element_type=f32).astype(bf16)


def _in_ch(x, mod, gpre, wt, tm):
    bsz, n, _ = x.shape
    ch = jax.ShapeDtypeStruct((bsz, M_WIDTH, n), bf16)
    ch_spec = pl.BlockSpec((1, M_WIDTH, tm), lambda b, i: (b, 0, i))
    return pl.pallas_call(
        _in_ch_kernel,
        grid=(bsz, n // tm),
        in_specs=[pl.BlockSpec((1, tm, D_MODEL), lambda b, i: (b, i, 0)),
                  pl.BlockSpec((1, 1, 3 * D_MODEL), lambda b, i: (b % mod.shape[0], 0, 0)),
                  pl.BlockSpec((1, D_MODEL), lambda b, i: (0, 0)),
                  pl.BlockSpec(wt.shape, lambda b, i: (0, 0))],
        out_specs=[ch_spec, ch_spec, ch_spec],
        out_shape=[ch, ch, ch],
        compiler_params=_params("parallel", "parallel"),
        name="in_proj_ab_ch",
    )(x, mod, gpre, wt)


def _log_sigmoid(a):
    return jnp.minimum(a, 0.0) - jnp.log1p(jnp.exp(-jnp.abs(a)))


def _gate_rows(g_ref, r_scr):
    g = g_ref[0]
    n = g.shape[0]
    sub = lax.broadcasted_iota(jnp.int32, (N_GATE, GATE_PAD), 0)
    lane = lax.broadcasted_iota(jnp.int32, (N_GATE, GATE_PAD), 1)
    pick = jnp.where(lane == (sub % 4) * M_HEADS + sub // 4, 1.0, 0.0).astype(bf16)
    rows = jnp.zeros((N_GATE, n), f32)
    rest = g
    for _ in range(3):
        piece = rest.astype(bf16)
        rest = rest - piece.astype(f32)
        rows = rows + lax.dot_general(pick, piece, NT_DIMS, preferred_element_type=f32)
    logf = _log_sigmoid(rows)
    pos = lax.broadcasted_iota(jnp.int32, (N_GATE, n), 1) % M_CHUNK
    pre, suf = logf, logf
    shift = 1
    while shift < M_CHUNK:
        pre = pre + jnp.where(pos >= shift, pltpu.roll(pre, shift, 1), 0.0)
        suf = suf + jnp.where(pos < M_CHUNK - shift, pltpu.roll(suf, n - shift, 1), 0.0)
        shift *= 2
    kind = lax.broadcasted_iota(jnp.int32, (N_GATE, n), 0) % 4
    final = jnp.where(kind == 1, pre, jnp.where(kind == 3, suf, rows))
    for h in range(M_HEADS):
        r_scr[h, 0:4, :] = final[4 * h:4 * h + 4]


STATE_ROWS = M_HEAD_DIM + BF16_ROWS


def _mlstm_chunk(seg, consts, state, start, direction, first):
    q_ref, k_ref, vt_ref, ot_ref, zt_ref, r_scr, acc, y_ref = seg
    mn_ref, head = consts
    ct_scr, m_scr = state
    if not isinstance(start, int):
        start = pl.multiple_of(start, M_CHUNK)
    span = pl.ds(start, M_CHUNK)
    dk = M_HEAD_DIM

    q = q_ref[0, span, :]
    k = k_ref[0, span, :]
    vt = vt_ref[0, :, span]

    li = r_scr[head, 2 * direction:2 * direction + 1, span]
    bcum = r_scr[head, 2 * direction + 1:2 * direction + 2, span]
    si = lax.broadcasted_iota(jnp.int32, (M_CHUNK, M_CHUNK), 0)
    ti = lax.broadcasted_iota(jnp.int32, (M_CHUNK, M_CHUNK), 1)
    causal = (si <= ti) if direction == 0 else (si >= ti)
    b_last = bcum[:, M_CHUNK - 1:M_CHUNK] if direction == 0 else bcum[:, 0:1]

    m_prev = m_scr[direction]
    ct = ct_scr[direction]
    n_mem = ct[dk:dk + 1, :]

    key_side = jnp.transpose(jnp.broadcast_to(li - bcum, (M_CHUNK, M_CHUNK)))
    dmat = jnp.where(causal, bcum + key_side, -jnp.inf)
    inter = bcum + m_prev
    m_t = jnp.maximum(inter, jnp.max(dmat, axis=0, keepdims=True))
    w_inter = jnp.exp(inter - m_t)
    s = lax.dot_general(k, q, NT_DIMS, preferred_element_type=f32) * jnp.exp(dmat - m_t)

    tail = lax.broadcasted_iota(jnp.int32, (BF16_ROWS, dk), 0)
    n_hi = n_mem.astype(bf16).astype(f32)
    extra = jnp.where(tail == 0, n_hi, jnp.where(tail == 1, n_mem - n_hi, 0.0))
    lhs = jnp.concatenate([ct[:dk].astype(bf16), extra.astype(bf16)], axis=0)
    from_state = lax.dot_general(lhs, q, NT_DIMS, preferred_element_type=f32)
    num = w_inter * from_state[:dk] + jnp.dot(vt, s.astype(bf16), preferred_element_type=f32)
    den = (w_inter * (from_state[dk:dk + 1] + from_state[dk + 1:dk + 2])
           + jnp.sum(s, axis=0, keepdims=True))
    h = num * (1.0 / jnp.maximum(jnp.abs(den), jnp.exp(-m_t)))

    d_last = b_last - bcum + li
    m_new = jnp.maximum(b_last + m_prev, jnp.max(d_last, axis=-1, keepdims=True))
    w_row = jnp.exp(d_last - m_new)
    decay = jnp.exp(b_last + m_prev - m_new)
    w_hi = w_row.astype(bf16).astype(f32)
    w_extra = jnp.where(tail[:, :M_CHUNK] == 0, w_hi, jnp.where(tail[:, :M_CHUNK] == 1, w_row - w_hi, 0.0))
    va = jnp.concatenate([(vt.astype(f32) * w_row).astype(bf16), w_extra.astype(bf16)], axis=0)
    update = jnp.dot(va, k, preferred_element_type=f32)
    ct_scr[direction, 0:dk, :] = decay * ct[:dk] + update[:dk]
    n_new = decay * n_mem + update[dk:dk + 1] + update[dk + 1:dk + 2]
    ct_scr[direction, dk:, :] = jnp.where(tail == 0, n_new, 0.0)
    m_scr[direction] = m_new

    if first:
        acc[:, span] = h
    else:
        hg = (acc[:, span] + h) * jax.nn.sigmoid(ot_ref[0, :, span].astype(f32))
        hn = hg * lax.rsqrt(jnp.mean(hg * hg, axis=0, keepdims=True) + EPS) * mn_ref[...]
        y_ref[0, :, span] = (hn * _silu(zt_ref[0, :, span].astype(f32))).astype(bf16)


def _mlstm_kernel(qx_ref, kx_ref, vx_ref, ox_ref, zx_ref, gx_ref,
                  qc_ref, kc_ref, vc_ref, oc_ref, zc_ref, gc_ref,
                  mn_ref, yx_ref, yc_ref,
                  rx_scr, rc_scr, accx, accc, ct_scr, m_scr):
    head = pl.program_id(1)
    nx, nc = qx_ref.shape[1], qc_ref.shape[1]

    @pl.when(head == 0)
    def _():
        _gate_rows(gx_ref, rx_scr)
        _gate_rows(gc_ref, rc_scr)

    ct_scr[...] = jnp.zeros_like(ct_scr)
    m_scr[...] = jnp.zeros_like(m_scr)

    consts = (mn_ref, head)
    state = (ct_scr, m_scr)
    ctx = (qc_ref, kc_ref, vc_ref, oc_ref, zc_ref, rc_scr, accc, yc_ref)
    lat = (qx_ref, kx_ref, vx_ref, ox_ref, zx_ref, rx_scr, accx, yx_ref)
    step = functools.partial(_mlstm_chunk, consts=consts, state=state)

    ncc = nc // M_CHUNK
    assert ncc == 2
    step(ctx, start=0, direction=0, first=True)
    step(ctx, start=M_CHUNK, direction=1, first=True)
    step(ctx, start=M_CHUNK, direction=0, first=False)
    step(ctx, start=0, direction=1, first=False)

    nxc = nx // M_CHUNK

    def meet(first):
        def body(i, carry):
            step(lat, start=i * M_CHUNK, direction=0, first=first)
            step(lat, start=(nxc - 1 - i) * M_CHUNK, direction=1, first=first)
            return carry
        return body

    lax.fori_loop(0, nxc // 2, meet(True), 0, unroll=8)
    lax.fori_loop(nxc // 2, nxc, meet(False), 0, unroll=4)


def _mlstm(lat, ctx, mnorm_tab):
    bsz, nx, _ = lat[0].shape
    nc = ctx[0].shape[1]
    hd = M_HEAD_DIM

    def specs(n):
        tok = pl.BlockSpec((1, n, hd), lambda b, h: (b, 0, h))
        ch = pl.BlockSpec((1, hd, n), lambda b, h: (b, h, 0))
        return [tok, tok, ch, ch, ch, pl.BlockSpec((1, n, GATE_PAD), lambda b, h: (b, 0, 0))]

    return pl.pallas_call(
        _mlstm_kernel,
        grid=(bsz, M_HEADS),
        in_specs=specs(nx) + specs(nc) + [pl.BlockSpec((hd, 128), lambda b, h: (h, 0))],
        out_specs=[pl.BlockSpec((1, hd, nx), lambda b, h: (b, h, 0)),
                   pl.BlockSpec((1, hd, nc), lambda b, h: (b, h, 0))],
        out_shape=[jax.ShapeDtypeStruct((bsz, M_WIDTH, nx), bf16),
                   jax.ShapeDtypeStruct((bsz, M_WIDTH, nc), bf16)],
        scratch_shapes=[pltpu.VMEM((M_HEADS, 8, nx), f32), pltpu.VMEM((M_HEADS, 8, nc), f32),
                        pltpu.VMEM((hd, nx), f32), pltpu.VMEM((hd, nc), f32),
                        pltpu.VMEM((2, STATE_ROWS, hd), f32), pltpu.VMEM((2, 1, 1), f32)],
        compiler_params=_params("parallel", "arbitrary"),
        name="mlstm",
    )(*lat, *ctx, mnorm_tab)


P_HALO = 16


def _out_ab_kernel(ym_ref, xp_ref, xpp_ref, xpn_ref, zp_ref, x_ref, mod_ref, gpost_ref,
                   wo_ref, pw_ref, ps_ref, o_ref, ext_scr):
    i, nt = pl.program_id(1), pl.num_programs(1)
    tm = x_ref.shape[1]
    ext_scr[0:P_HALO, :] = jnp.where(i > 0, xpp_ref[0].astype(f32), 0.0)
    ext_scr[P_HALO:P_HALO + tm, :] = xp_ref[0].astype(f32)
    ext_scr[P_HALO + tm:, :] = jnp.where(i < nt - 1, xpn_ref[0].astype(f32), 0.0)

    n = nt * tm
    ext_rows = tm + 2 * P_HALO
    t = i * tm + lax.broadcasted_iota(jnp.int32, (tm, 1), 0)
    y = jnp.dot(ym_ref[0].T, wo_ref[0:M_WIDTH, :], preferred_element_type=f32)
    for g, win in enumerate(P_WINDOWS):
        half = win // 2
        cols = slice(g * P_GROUP_DIM, (g + 1) * P_GROUP_DIM)
        e = ext_scr[:, cols]
        run, width = e, 1
        while width < win:
            run = run + pltpu.roll(run, ext_rows - width, 0)
            width *= 2
        total = pltpu.roll(run, half, 0)[P_HALO:P_HALO + tm]
        count = (jnp.minimum(t + half, n) - jnp.maximum(t - half, 0)).astype(f32)
        pooled = total * (1.0 / count) - e[P_HALO:P_HALO + tm]
        mixed = jnp.dot(pooled.astype(bf16), pw_ref[g], preferred_element_type=f32) * ps_ref[:, cols]
        yp = (mixed * _silu(zp_ref[0, :, cols].astype(f32))).astype(bf16)
        y = y + jnp.dot(yp, wo_ref[M_WIDTH + g * P_GROUP_DIM:M_WIDTH + (g + 1) * P_GROUP_DIM, :],
                        preferred_element_type=f32)
    o_ref[0] = _gated_residual(x_ref[0], y, gpost_ref[...], mod_ref[0])


def _out_ab(ymt, xp, zp, x, mod, gpost, wo, pw, ps, tm):
    bsz, n, _ = x.shape
    nh = n // P_HALO
    th = tm // P_HALO
    return pl.pallas_call(
        _out_ab_kernel,
        grid=(bsz, n // tm),
        in_specs=[pl.BlockSpec((1, M_WIDTH, tm), lambda b, i: (b, 0, i)),
                  pl.BlockSpec((1, tm, P_WIDTH), lambda b, i: (b, i, 0)),
                  pl.BlockSpec((1, P_HALO, P_WIDTH), lambda b, i: (b, jnp.maximum(i * th - 1, 0), 0)),
                  pl.BlockSpec((1, P_HALO, P_WIDTH), lambda b, i: (b, jnp.minimum((i + 1) * th, nh - 1), 0)),
                  pl.BlockSpec((1, tm, P_WIDTH), lambda b, i: (b, i, 0)),
                  pl.BlockSpec((1, tm, D_MODEL), lambda b, i: (b, i, 0)),
                  pl.BlockSpec((1, 1, 3 * D_MODEL), lambda b, i: (b % mod.shape[0], 0, 0)),
                  pl.BlockSpec((1, D_MODEL), lambda b, i: (0, 0)),
                  pl.BlockSpec((M_WIDTH + P_WIDTH, D_MODEL), lambda b, i: (0, 0)),
                  pl.BlockSpec((len(P_WINDOWS), P_GROUP_DIM, P_GROUP_DIM), lambda b, i: (0, 0, 0)),
                  pl.BlockSpec((1, P_WIDTH), lambda b, i: (0, 0))],
        out_specs=pl.BlockSpec((1, tm, D_MODEL), lambda b, i: (b, i, 0)),
        out_shape=jax.ShapeDtypeStruct((bsz, n, D_MODEL), f32),
        scratch_shapes=[pltpu.VMEM((tm + 2 * P_HALO, P_WIDTH), f32)],
        compiler_params=_params("parallel", "parallel"),
        name="out_proj_ab",
    )(ymt, xp, xp, xp, zp, x, mod, gpost, wo, pw, ps)


def _rope(a, cos, sin):
    lane = lax.broadcasted_iota(jnp.int32, a.shape, 1)
    half = A_HEAD_DIM // 2
    partner = jnp.where(lane % A_HEAD_DIM < half,
                        pltpu.roll(a, a.shape[1] - half, 1), pltpu.roll(a, half, 1))
    return a * cos + partner * sin


LOG2E = 1.4426950408889634


def _in_c_kernel(x_ref, mod_ref, gpre_ref, wt_ref, wk_ref, cost_ref, sint_ref, cos_ref, sin_ref,
                 qt_ref, k_ref, vt_ref, zt_ref, *, rope):
    half = A_HEAD_DIM // 2
    q_scale = (A_HEAD_DIM ** -0.5) * LOG2E
    for t0 in range(0, x_ref.shape[1], SUB_ROWS):
        toks = slice(t0, t0 + SUB_ROWS)
        hb = _modulated_norm(x_ref[0, toks, :], gpre_ref[...], mod_ref[0]).astype(bf16)

        def channel_major(lo, rows):
            return lax.dot_general(wt_ref[lo:lo + rows, :], hb, NT_DIMS, preferred_element_type=f32)

        for lo in range(0, A_WIDTH, C_ROWS):
            a = channel_major(lo, C_ROWS)
            for h in range(C_ROWS // A_HEAD_DIM):
                x1 = a[h * A_HEAD_DIM:h * A_HEAD_DIM + half]
                x2 = a[h * A_HEAD_DIM + half:(h + 1) * A_HEAD_DIM]
                if rope:
                    c, s = cost_ref[:, toks], sint_ref[:, toks]
                    x1, x2 = x1 * c - x2 * s, x1 * s + x2 * c
                r0 = lo + h * A_HEAD_DIM
                qt_ref[0, r0:r0 + half, toks] = (x1 * q_scale).astype(bf16)
                qt_ref[0, r0 + half:r0 + A_HEAD_DIM, toks] = (x2 * q_scale).astype(bf16)
        vt_ref[0, :, toks] = channel_major(A_WIDTH, A_KV_WIDTH).astype(bf16)
        for lo in range(0, A_WIDTH, C_ROWS):
            zt_ref[0, lo:lo + C_ROWS, toks] = channel_major(A_WIDTH + A_KV_WIDTH + lo, C_ROWS).astype(bf16)
        for j in range(A_KV_WIDTH // 128):
            kk = jnp.dot(hb, wk_ref[:, j * 128:(j + 1) * 128], preferred_element_type=f32)
            if rope:
                kk = _rope(kk, cos_ref[toks, :], sin_ref[toks, :])
            k_ref[0, toks, j * 128:(j + 1) * 128] = kk.astype(bf16)


def _in_c(x, mod, gpre, wt, wk, cost, sint, cos, sin, tm, rope):
    bsz, n, _ = x.shape
    half = A_HEAD_DIM // 2
    return pl.pallas_call(
        functools.partial(_in_c_kernel, rope=rope),
        grid=(bsz, n // tm),
        in_specs=[pl.BlockSpec((1, tm, D_MODEL), lambda b, i: (b, i, 0)),
                  pl.BlockSpec((1, 1, 3 * D_MODEL), lambda b, i: (b % mod.shape[0], 0, 0)),
                  pl.BlockSpec((1, D_MODEL), lambda b, i: (0, 0)),
                  pl.BlockSpec(wt.shape, lambda b, i: (0, 0)),
                  pl.BlockSpec(wk.shape, lambda b, i: (0, 0)),
                  pl.BlockSpec((half, tm), lambda b, i: (0, i)),
                  pl.BlockSpec((half, tm), lambda b, i: (0, i)),
                  pl.BlockSpec((tm, 128), lambda b, i: (i, 0)),
                  pl.BlockSpec((tm, 128), lambda b, i: (i, 0))],
        out_specs=[pl.BlockSpec((1, A_WIDTH, tm), lambda b, i: (b, 0, i)),
                   pl.BlockSpec((1, tm, A_KV_WIDTH), lambda b, i: (b, i, 0)),
                   pl.BlockSpec((1, A_KV_WIDTH, tm), lambda b, i: (b, 0, i)),
                   pl.BlockSpec((1, A_WIDTH, tm), lambda b, i: (b, 0, i))],
        out_shape=[jax.ShapeDtypeStruct((bsz, A_WIDTH, n), bf16),
                   jax.ShapeDtypeStruct((bsz, n, A_KV_WIDTH), bf16),
                   jax.ShapeDtypeStruct((bsz, A_KV_WIDTH, n), bf16),
                   jax.ShapeDtypeStruct((bsz, A_WIDTH, n), bf16)],
        compiler_params=_params("parallel", "parallel"),
        name="in_proj_c",
    )(x, mod, gpre, wt, wk, cost, sint, cos, sin)


def _attn_kernel(sink_ref, qt_ref, k_ref, vt_ref, kc_ref, vct_ref, zt_ref, ogt_ref, s_scr, *, banded):
    nc = kc_ref.shape[1]
    blocks = qt_ref.shape[2] // A_BLOCK
    band_start, bias = [], []
    if banded:
        n = k_ref.shape[1]
        for qb in range(blocks):
            start = (pl.program_id(1) * blocks + qb) * A_BLOCK
            bs = pl.multiple_of(jnp.clip(start - A_WINDOW, 0, n - A_BAND), A_BLOCK)
            kj = bs + lax.broadcasted_iota(jnp.int32, (A_BAND, A_BLOCK), 0)
            qi = start + lax.broadcasted_iota(jnp.int32, (A_BAND, A_BLOCK), 1)
            mask = jnp.where(jnp.abs(qi - kj) <= A_WINDOW, 0.0, -jnp.inf)
            band_start.append(bs)
            bias.append(jnp.concatenate([mask] * A_GROUP, axis=1))

    def scores(u):
        qb, g = divmod(u, A_KV_HEADS)
        lanes = slice(qb * A_BLOCK, (qb + 1) * A_BLOCK)
        heads = range(g * A_GROUP, (g + 1) * A_GROUP)
        pair = slice(128 * (g // 2), 128 * (g // 2) + 128)
        qg = jnp.concatenate([qt_ref[0, h * A_HEAD_DIM:(h + 1) * A_HEAD_DIM, lanes] for h in heads], axis=1)
        zero = jnp.zeros_like(qg)
        qpad = jnp.concatenate([qg, zero] if g % 2 == 0 else [zero, qg], axis=0)
        slot = u % s_scr.shape[0]
        s_scr[slot, 0:nc, :] = jnp.dot(kc_ref[0, :, pair], qpad, preferred_element_type=f32)
        if banded:
            s_scr[slot, nc:, :] = jnp.dot(k_ref[0, pl.ds(band_start[qb], A_BAND), pair], qpad,
                                       preferred_element_type=f32) + bias[qb]

    def softmax(u):
        g = u % A_KV_HEADS
        heads = range(g * A_GROUP, (g + 1) * A_GROUP)
        sk = jnp.concatenate([jnp.full((1, A_BLOCK), sink_ref[h] * LOG2E, f32) for h in heads], axis=1)
        s = s_scr[u % s_scr.shape[0]]
        m = jnp.maximum(jnp.max(s, axis=0, keepdims=True), sk)
        p = jnp.exp2(s - m)
        inv = 1.0 / (jnp.sum(p, axis=0, keepdims=True) + jnp.exp2(sk - m))
        return p.astype(bf16), inv

    def output(u, pb, inv):
        qb, g = divmod(u, A_KV_HEADS)
        lanes = slice(qb * A_BLOCK, (qb + 1) * A_BLOCK)
        heads = range(g * A_GROUP, (g + 1) * A_GROUP)
        kvrows = slice(g * A_HEAD_DIM, (g + 1) * A_HEAD_DIM)
        ot = jnp.dot(vct_ref[0, kvrows, :], pb[0:nc], preferred_element_type=f32)
        if banded:
            ot = ot + jnp.dot(vt_ref[0, kvrows, pl.ds(band_start[qb], A_BAND)], pb[nc:],
                              preferred_element_type=f32)
        ot = ot * inv
        for a, h in enumerate(heads):
            rows = slice(h * A_HEAD_DIM, (h + 1) * A_HEAD_DIM)
            gate = _silu(zt_ref[0, rows, lanes].astype(f32))
            ogt_ref[0, rows, lanes] = (ot[:, a * A_BLOCK:(a + 1) * A_BLOCK] * gate).astype(bf16)

    units = blocks * A_KV_HEADS
    scores(0)
    scores(1)
    pending = None
    for u in range(units):
        sm = softmax(u)
        if u + 2 < units:
            scores(u + 2)
        if pending is not None:
            output(u - 1, *pending)
        pending = sm
    output(units - 1, *pending)


def _attn(sink, qt, k, vt, kc, vct, zt, banded):
    bsz, _, nq = qt.shape
    nk, nc = k.shape[1], kc.shape[1]
    nkeys = nc + (A_BAND if banded else 0)
    blocks = min(A_QUERY_BLOCKS, nq // A_BLOCK)
    tq = blocks * A_BLOCK
    return pl.pallas_call(
        functools.partial(_attn_kernel, banded=banded),
        grid=(bsz, nq // tq),
        in_specs=[pl.BlockSpec(memory_space=pltpu.SMEM),
                  pl.BlockSpec((1, A_WIDTH, tq), lambda b, i: (b, 0, i)),
                  pl.BlockSpec((1, nk, A_KV_WIDTH), lambda b, i: (b, 0, 0)),
                  pl.BlockSpec((1, A_KV_WIDTH, nk), lambda b, i: (b, 0, 0)),
                  pl.BlockSpec((1, nc, A_KV_WIDTH), lambda b, i: (b, 0, 0)),
                  pl.BlockSpec((1, A_KV_WIDTH, nc), lambda b, i: (b, 0, 0)),
                  pl.BlockSpec((1, A_WIDTH, tq), lambda b, i: (b, 0, i))],
        out_specs=pl.BlockSpec((1, A_WIDTH, tq), lambda b, i: (b, 0, i)),
        out_shape=jax.ShapeDtypeStruct((bsz, A_WIDTH, nq), bf16),
        scratch_shapes=[pltpu.VMEM((4, nkeys, A_GROUP * A_BLOCK), f32)],
        compiler_params=_params("parallel", "parallel"),
        name="attention_banded" if banded else "attention_context",
    )(sink, qt, k, vt, kc, vct, zt)


def _out_c_kernel(at_ref, x_ref, mod_ref, gpost_ref, wo_ref, o_ref):
    y = jnp.dot(at_ref[0].T, wo_ref[...], preferred_element_type=f32)
    o_ref[0] = _gated_residual(x_ref[0], y, gpost_ref[...], mod_ref[0])


def _out_c(at, x, mod, gpost, wo, tm):
    bsz, n, _ = x.shape
    return pl.pallas_call(
        _out_c_kernel,
        grid=(bsz, n // tm),
        in_specs=[pl.BlockSpec((1, A_WIDTH, tm), lambda b, i: (b, 0, i)),
                  pl.BlockSpec((1, tm, D_MODEL), lambda b, i: (b, i, 0)),
                  pl.BlockSpec((1, 1, 3 * D_MODEL), lambda b, i: (b % mod.shape[0], 0, 0)),
                  pl.BlockSpec((1, D_MODEL), lambda b, i: (0, 0)),
                  pl.BlockSpec((A_WIDTH, D_MODEL), lambda b, i: (0, 0))],
        out_specs=pl.BlockSpec((1, tm, D_MODEL), lambda b, i: (b, i, 0)),
        out_shape=jax.ShapeDtypeStruct((bsz, n, D_MODEL), f32),
        compiler_params=_params("parallel", "parallel"),
        name="out_proj_c",
    )(at, x, mod, gpost, wo)


def _rope_tables(n):
    rows = n // GRID_W
    row = np.repeat(np.arange(rows), GRID_W).astype(np.float64)
    col = np.tile(np.arange(GRID_W), rows).astype(np.float64)
    n_freq = A_HEAD_DIM // 4
    inv = ROPE_BASE ** (-np.arange(n_freq, dtype=np.float64) / n_freq)
    ang = np.concatenate([row[:, None] * inv, col[:, None] * inv], -1)
    cos, sin = np.cos(ang).astype(np.float32), np.sin(ang).astype(np.float32)
    cos_t = np.tile(np.concatenate([cos, cos], -1), (1, 128 // A_HEAD_DIM))
    sin_t = np.tile(np.concatenate([-sin, sin], -1), (1, 128 // A_HEAD_DIM))
    return tuple(jnp.asarray(np.ascontiguousarray(a)) for a in (cos.T, sin.T, cos_t, sin_t))


def _split_pairs_columns(w, heads):
    perm = np.concatenate([np.arange(0, A_HEAD_DIM, 2), np.arange(1, A_HEAD_DIM, 2)])
    idx = (np.arange(heads)[:, None] * A_HEAD_DIM + perm[None, :]).reshape(-1)
    return w[:, idx]


def kernel(x, c, ctx, c_ctx, w_mod, b_mod, g_pre, g_post, ab_w_in, ab_b_gate, ab_conv, ab_mnorm,
           ab_pool_w, ab_pool_scale, ab_w_out, c_w_in, c_sink, c_w_out):
    bsz, n, _ = x.shape
    n_ctx = ctx.shape[1]
    assert bsz < 16
    cs = jnp.zeros((16, D_MODEL), f32).at[:bsz].set(c).at[bsz].set(c_ctx)
    mod_all = _modulation(cs, w_mod, b_mod)
    cost, sint, cos, sin = _rope_tables(n)
    tm_x = 1024

    for l in range(DEPTH):
        j = l // 2
        last = l == DEPTH - 1
        mod_x = mod_all[l, :bsz].reshape(bsz, 1, 3 * D_MODEL)
        mod_c = mod_all[l, bsz:bsz + 1].reshape(1, 1, 3 * D_MODEL)
        gpre = g_pre[l].reshape(1, D_MODEL)
        gpost = g_post[l].reshape(1, D_MODEL)
        if l % 2 == 0:
            w = ab_w_in[j].astype(bf16)
            w_tok = jnp.concatenate([w[:, :2 * M_WIDTH], w[:, 5 * M_WIDTH:GATE_OFF]], axis=1)
            wt_ch = w[:, 2 * M_WIDTH:5 * M_WIDTH].T
            wg = jnp.pad(w[:, GATE_OFF:], ((0, 0), (0, GATE_PAD - N_GATE)))
            bg = jnp.pad(ab_b_gate[j], (0, GATE_PAD - N_GATE)).reshape(1, GATE_PAD)
            k_scale = jnp.concatenate([jnp.ones((M_WIDTH,), f32), jnp.full((M_WIDTH,), M_HEAD_DIM ** -0.5, f32)])
            cw = ab_conv[j] * k_scale
            mnorm_tab = jnp.broadcast_to(ab_mnorm[j][:, None], (M_WIDTH, 128))

            def project(a, mod_a, tm_tok, tm_ch):
                q, k, xp, zp, g = _in_tok(a, mod_a, gpre, w_tok, wg, bg, cw, tm_tok)
                vt, ot, zt = _in_ch(a, mod_a, gpre, wt_ch, tm_ch)
                return (q, k, vt, ot, zt, g), xp, zp

            lat, xpx, zpx = project(x, mod_x, 1024, 1024)
            con, xpc, zpc = project(ctx, mod_c, n_ctx, n_ctx)
            ymx, ymc = _mlstm(lat, con, mnorm_tab)
            wo = ab_w_out[j].astype(bf16)
            pw = ab_pool_w[j].astype(bf16)
            ps = ab_pool_scale[j].reshape(1, P_WIDTH)
            x = _out_ab(ymx, xpx, zpx, x, mod_x, gpost, wo, pw, ps, 512)
            if not last:
                ctx = _out_ab(ymc, xpc, zpc, ctx, mod_c, gpost, wo, pw, ps, n_ctx)
        else:
            w = c_w_in[j].astype(bf16)
            wt = jnp.concatenate([_split_pairs_columns(w[:, :A_WIDTH], A_HEADS),
                                  w[:, A_WIDTH + A_KV_WIDTH:]], axis=1).T
            wk = _split_pairs_columns(w[:, A_WIDTH:A_WIDTH + A_KV_WIDTH], A_KV_HEADS)
            qx, kx, vx, zx = _in_c(x, mod_x, gpre, wt, wk, cost, sint, cos, sin, 1024, True)
            qc, kc, vc, zc = _in_c(ctx, mod_c, gpre, wt, wk, cost[:, :n_ctx], sint[:, :n_ctx],
                                   cos[:n_ctx], sin[:n_ctx], n_ctx, False)
            wo = c_w_out[j].astype(bf16)
            ax = _attn(c_sink[j], qx, kx, vx, kc, vc, zx, True)
            x = _out_c(ax, x, mod_x, gpost, wo, tm_x)
            if not last:
                ac = _attn(c_sink[j], qc, kc, vc, kc, vc, zc, False)
                ctx = _out_c(ac, ctx, mod_c, gpost, wo, n_ctx)
    return x
```

```python
import functools

import jax
import jax.numpy as jnp
import numpy as np
from jax import lax
from jax.experimental import pallas as pl
from jax.experimental.pallas import tpu as pltpu

f32 = jnp.float32
bf16 = jnp.bfloat16

D_MODEL = 1024
DEPTH = 4
EPS = 1e-6
GRID_W = 64

M_HEADS = 4
M_HEAD_DIM = 256
M_WIDTH = 1024
M_CHUNK = 128
P_WINDOWS = (2, 4, 8, 16)
P_GROUP_DIM = 256
P_WIDTH = 1024
N_GATE = 16
GATE_OFF = 5 * M_WIDTH + 2 * P_WIDTH
GATE_PAD = 128

A_HEADS = 16
A_KV_HEADS = 4
A_GROUP = 4
A_HEAD_DIM = 64
A_WIDTH = 1024
A_KV_WIDTH = 256
A_WINDOW = 128
A_BLOCK = 128
A_BAND = A_BLOCK + 2 * A_WINDOW
A_QUERY_BLOCKS = 4
ROPE_BASE = 10000.0

BF16_ROWS = 16
VMEM_LIMIT = 52 * 1024 * 1024


def _params(*sem):
    return pltpu.CompilerParams(dimension_semantics=sem, vmem_limit_bytes=VMEM_LIMIT)


def _silu(a):
    return a * jax.nn.sigmoid(a)


def _modulated_norm(x, gpre, mod):
    xn = x * lax.rsqrt(jnp.mean(x * x, -1, keepdims=True) + EPS) * gpre
    return xn * (1.0 + mod[:, D_MODEL:2 * D_MODEL]) + mod[:, :D_MODEL]


def _gated_residual(x, y, gpost, mod):
    r = y * lax.rsqrt(jnp.mean(y * y, -1, keepdims=True) + EPS) * gpost
    return x + mod[:, 2 * D_MODEL:] * r


def _mod_kernel(cs_ref, w_ref, b_ref, o_ref):
    a = _silu(cs_ref[...])
    o_ref[0] = jnp.dot(a, w_ref[0], preferred_element_type=f32,
                       precision=lax.Precision.HIGHEST) + b_ref[0]


def _modulation(cs, w_mod, b_mod):
    tn = 1024
    return pl.pallas_call(
        _mod_kernel,
        grid=(DEPTH, 3 * D_MODEL // tn),
        in_specs=[pl.BlockSpec((16, D_MODEL), lambda l, j: (0, 0)),
                  pl.BlockSpec((1, D_MODEL, tn), lambda l, j: (l, 0, j)),
                  pl.BlockSpec((1, 1, tn), lambda l, j: (l, 0, j))],
        out_specs=pl.BlockSpec((1, 16, tn), lambda l, j: (l, 0, j)),
        out_shape=jax.ShapeDtypeStruct((DEPTH, 16, 3 * D_MODEL), f32),
        compiler_params=_params("parallel", "parallel"),
        name="modulation",
    )(cs, w_mod, b_mod.reshape(DEPTH, 1, 3 * D_MODEL))


NT_DIMS = (((1,), (1,)), ((), ()))
C_ROWS = 512
HALO = 16
CONV_COLS = 256
SUB_ROWS = 256


def _in_tok_kernel(x_ref, xprev_ref, xnext_ref, mod_ref, gpre_ref, w_ref, wxp_ref, wzp_ref, wg_ref, bg_ref,
                   cw_ref, q_ref, k_ref, xp_ref, zp_ref, g_ref, h_scr):
    i, nt, j = pl.program_id(1), pl.num_programs(1), pl.program_id(2)
    tm = x_ref.shape[1]

    @pl.when(j == 0)
    def _():
        def normed(a):
            return _modulated_norm(a, gpre_ref[...], mod_ref[0]).astype(bf16)

        h_scr[0:HALO, :] = jnp.where(i > 0, normed(xprev_ref[0]), 0.0).astype(bf16)
        h_scr[HALO:HALO + tm, :] = normed(x_ref[0])
        h_scr[HALO + tm:, :] = jnp.where(i < nt - 1, normed(xnext_ref[0]), 0.0).astype(bf16)
        g_ref[0] = jnp.dot(h_scr[HALO:HALO + tm, :], wg_ref[...], preferred_element_type=f32) + bg_ref[...]

        ext = tm + 2 * HALO
        for half, out_ref in enumerate((q_ref, k_ref)):
            for lo in range(0, M_WIDTH, CONV_COLS):
                src = slice(half * M_WIDTH + lo, half * M_WIDTH + lo + CONV_COLS)
                r = jnp.dot(h_scr[...], w_ref[:, src], preferred_element_type=f32)
                conv = (cw_ref[0:1, src] * pltpu.roll(r, 1, 0) + cw_ref[1:2, src] * r
                        + cw_ref[2:3, src] * pltpu.roll(r, ext - 1, 0))
                out_ref[0, :, lo:lo + CONV_COLS] = conv[HALO:HALO + tm].astype(bf16)

    @pl.when(j == 1)
    def _():
        for out_ref, wp_ref in ((xp_ref, wxp_ref), (zp_ref, wzp_ref)):
            for lo in range(0, P_WIDTH, CONV_COLS):
                out_ref[0, :, lo:lo + CONV_COLS] = jnp.dot(
                    h_scr[HALO:HALO + tm, :], wp_ref[:, lo:lo + CONV_COLS],
                    preferred_element_type=f32).astype(bf16)


def _in_tok(x, mod, gpre, w, wg, bg, cw, tm):
    bsz, n, _ = x.shape
    th, nh = tm // HALO, n // HALO
    tok = jax.ShapeDtypeStruct((bsz, n, M_WIDTH), bf16)
    tok_spec = pl.BlockSpec((1, tm, M_WIDTH), lambda b, i, j: (b, i, 0))
    xp_block = 5 * M_WIDTH // P_WIDTH
    return pl.pallas_call(
        _in_tok_kernel,
        grid=(bsz, n // tm, 2),
        in_specs=[pl.BlockSpec((1, tm, D_MODEL), lambda b, i, j: (b, i, 0)),
                  pl.BlockSpec((1, HALO, D_MODEL), lambda b, i, j: (b, jnp.maximum(i * th - 1, 0), 0)),
                  pl.BlockSpec((1, HALO, D_MODEL), lambda b, i, j: (b, jnp.minimum((i + 1) * th, nh - 1), 0)),
                  pl.BlockSpec((1, 1, 3 * D_MODEL), lambda b, i, j: (b % mod.shape[0], 0, 0)),
                  pl.BlockSpec((1, D_MODEL), lambda b, i, j: (0, 0)),
                  pl.BlockSpec((D_MODEL, 2 * M_WIDTH), lambda b, i, j: (0, 0)),
                  pl.BlockSpec((D_MODEL, P_WIDTH), lambda b, i, j: (0, xp_block)),
                  pl.BlockSpec((D_MODEL, P_WIDTH), lambda b, i, j: (0, xp_block + 1)),
                  pl.BlockSpec((D_MODEL, GATE_PAD), lambda b, i, j: (0, 0)),
                  pl.BlockSpec((1, GATE_PAD), lambda b, i, j: (0, 0)),
                  pl.BlockSpec((3, 2 * M_WIDTH), lambda b, i, j: (0, 0))],
        out_specs=[tok_spec, tok_spec, tok_spec, tok_spec,
                   pl.BlockSpec((1, tm, GATE_PAD), lambda b, i, j: (b, i, 0))],
        out_shape=[tok, tok, tok, tok, jax.ShapeDtypeStruct((bsz, n, GATE_PAD), f32)],
        scratch_shapes=[pltpu.VMEM((tm + 2 * HALO, D_MODEL), bf16)],
        compiler_params=_params("parallel", "parallel", "arbitrary"),
        name="in_proj_ab_tok",
    )(x, x, x, mod, gpre, w, w, w, wg, bg, cw)


def _in_ch_kernel(x_ref, mod_ref, gpre_ref, wt_ref, vt_ref, ot_ref, zt_ref):
    for t0 in range(0, x_ref.shape[1], SUB_ROWS):
        toks = slice(t0, t0 + SUB_ROWS)
        hb = _modulated_norm(x_ref[0, toks, :], gpre_ref[...], mod_ref[0]).astype(bf16)
        for idx, out_ref in enumerate((vt_ref, ot_ref, zt_ref)):
            for lo in range(0, M_WIDTH, C_ROWS):
                rows = wt_ref[idx * M_WIDTH + lo:idx * M_WIDTH + lo + C_ROWS, :]
                out_ref[0, lo:lo + C_ROWS, toks] = lax.dot_general(
                    rows, hb, NT_DIMS, preferred_element_type=f32).astype(bf16)


def _in_ch(x, mod, gpre, wt, tm):
    bsz, n, _ = x.shape
    ch = jax.ShapeDtypeStruct((bsz, M_WIDTH, n), bf16)
    ch_spec = pl.BlockSpec((1, M_WIDTH, tm), lambda b, i: (b, 0, i))
    return pl.pallas_call(
        _in_ch_kernel,
        grid=(bsz, n // tm),
        in_specs=[pl.BlockSpec((1, tm, D_MODEL), lambda b, i: (b, i, 0)),
                  pl.BlockSpec((1, 1, 3 * D_MODEL), lambda b, i: (b % mod.shape[0], 0, 0)),
                  pl.BlockSpec((1, D_MODEL), lambda b, i: (0, 0)),
                  pl.BlockSpec(wt.shape, lambda b, i: (0, 0))],
        out_specs=[ch_spec, ch_spec, ch_spec],
        out_shape=[ch, ch, ch],
        compiler_params=_params("parallel", "parallel"),
        name="in_proj_ab_ch",
    )(x, mod, gpre, wt)


def _log_sigmoid(a):
    return jnp.minimum(a, 0.0) - jnp.log1p(jnp.exp(-jnp.abs(a)))


def _gate_rows(g_ref, r_scr):
    g = g_ref[0]
    n = g.shape[0]
    sub = lax.broadcasted_iota(jnp.int32, (N_GATE, GATE_PAD), 0)
    lane = lax.broadcasted_iota(jnp.int32, (N_GATE, GATE_PAD), 1)
    pick = jnp.where(lane == (sub % 4) * M_HEADS + sub // 4, 1.0, 0.0).astype(bf16)
    rows = jnp.zeros((N_GATE, n), f32)
    rest = g
    for _ in range(3):
        piece = rest.astype(bf16)
        rest = rest - piece.astype(f32)
        rows = rows + lax.dot_general(pick, piece, NT_DIMS, preferred_element_type=f32)
    logf = _log_sigmoid(rows)
    pos = lax.broadcasted_iota(jnp.int32, (N_GATE, n), 1) % M_CHUNK
    pre, suf = logf, logf
    shift = 1
    while shift < M_CHUNK:
        pre = pre + jnp.where(pos >= shift, pltpu.roll(pre, shift, 1), 0.0)
        suf = suf + jnp.where(pos < M_CHUNK - shift, pltpu.roll(suf, n - shift, 1), 0.0)
        shift *= 2
    kind = lax.broadcasted_iota(jnp.int32, (N_GATE, n), 0) % 4
    final = jnp.where(kind == 1, pre, jnp.where(kind == 3, suf, rows))
    for h in range(M_HEADS):
        r_scr[h, 0:4, :] = final[4 * h:4 * h + 4]


STATE_ROWS = M_HEAD_DIM + BF16_ROWS


def _mlstm_chunk(seg, consts, state, start, direction, first):
    q_ref, k_ref, vt_ref, ot_ref, zt_ref, r_scr, acc, y_ref = seg
    mn_ref, head = consts
    ct_scr, m_scr = state
    if not isinstance(start, int):
        start = pl.multiple_of(start, M_CHUNK)
    span = pl.ds(start, M_CHUNK)
    dk = M_HEAD_DIM

    q = q_ref[0, span, :]
    k = k_ref[0, span, :]
    vt = vt_ref[0, :, span]

    li = r_scr[head, 2 * direction:2 * direction + 1, span]
    bcum = r_scr[head, 2 * direction + 1:2 * direction + 2, span]
    si = lax.broadcasted_iota(jnp.int32, (M_CHUNK, M_CHUNK), 0)
    ti = lax.broadcasted_iota(jnp.int32, (M_CHUNK, M_CHUNK), 1)
    causal = (si <= ti) if direction == 0 else (si >= ti)
    b_last = bcum[:, M_CHUNK - 1:M_CHUNK] if direction == 0 else bcum[:, 0:1]

    m_prev = m_scr[direction]
    ct = ct_scr[direction]
    n_mem = ct[dk:dk + 1, :]

    key_side = jnp.transpose(jnp.broadcast_to(li - bcum, (M_CHUNK, M_CHUNK)))
    dmat = jnp.where(causal, bcum + key_side, -jnp.inf)
    inter = bcum + m_prev
    m_t = jnp.maximum(inter, jnp.max(dmat, axis=0, keepdims=True))
    w_inter = jnp.exp(inter - m_t)
    s = lax.dot_general(k, q, NT_DIMS, preferred_element_type=f32) * jnp.exp(dmat - m_t)

    tail = lax.broadcasted_iota(jnp.int32, (BF16_ROWS, dk), 0)
    n_hi = n_mem.astype(bf16).astype(f32)
    extra = jnp.where(tail == 0, n_hi, jnp.where(tail == 1, n_mem - n_hi, 0.0))
    lhs = jnp.concatenate([ct[:dk].astype(bf16), extra.astype(bf16)], axis=0)
    from_state = lax.dot_general(lhs, q, NT_DIMS, preferred_element_type=f32)
    num = w_inter * from_state[:dk] + jnp.dot(vt, s.astype(bf16), preferred_element_type=f32)
    den = (w_inter * (from_state[dk:dk + 1] + from_state[dk + 1:dk + 2])
           + jnp.sum(s, axis=0, keepdims=True))
    h = num * (1.0 / jnp.maximum(jnp.abs(den), jnp.exp(-m_t)))

    d_last = b_last - bcum + li
    m_new = jnp.maximum(b_last + m_prev, jnp.max(d_last, axis=-1, keepdims=True))
    w_row = jnp.exp(d_last - m_new)
    decay = jnp.exp(b_last + m_prev - m_new)
    w_hi = w_row.astype(bf16).astype(f32)
    w_extra = jnp.where(tail[:, :M_CHUNK] == 0, w_hi, jnp.where(tail[:, :M_CHUNK] == 1, w_row - w_hi, 0.0))
    va = jnp.concatenate([(vt.astype(f32) * w_row).astype(bf16), w_extra.astype(bf16)], axis=0)
    update = jnp.dot(va, k, preferred_element_type=f32)
    ct_scr[direction, 0:dk, :] = decay * ct[:dk] + update[:dk]
    n_new = decay * n_mem + update[dk:dk + 1] + update[dk + 1:dk + 2]
    ct_scr[direction, dk:, :] = jnp.where(tail == 0, n_new, 0.0)
    m_scr[direction] = m_new

    if first:
        acc[:, span] = h
    else:
        hg = (acc[:, span] + h) * jax.nn.sigmoid(ot_ref[0, :, span].astype(f32))
        hn = hg * lax.rsqrt(jnp.mean(hg * hg, axis=0, keepdims=True) + EPS) * mn_ref[...]
        y_ref[0, :, span] = (hn * _silu(zt_ref[0, :, span].astype(f32))).astype(bf16)


def _mlstm_kernel(qx_ref, kx_ref, vx_ref, ox_ref, zx_ref, gx_ref,
                  qc_ref, kc_ref, vc_ref, oc_ref, zc_ref, gc_ref,
                  mn_ref, yx_ref, yc_ref,
                  rx_scr, rc_scr, accx, accc, ct_scr, m_scr):
    head = pl.program_id(1)
    nx, nc = qx_ref.shape[1], qc_ref.shape[1]

    @pl.when(head == 0)
    def _():
        _gate_rows(gx_ref, rx_scr)
        _gate_rows(gc_ref, rc_scr)

    ct_scr[...] = jnp.zeros_like(ct_scr)
    m_scr[...] = jnp.zeros_like(m_scr)

    consts = (mn_ref, head)
    state = (ct_scr, m_scr)
    ctx = (qc_ref, kc_ref, vc_ref, oc_ref, zc_ref, rc_scr, accc, yc_ref)
    lat = (qx_ref, kx_ref, vx_ref, ox_ref, zx_ref, rx_scr, accx, yx_ref)
    step = functools.partial(_mlstm_chunk, consts=consts, state=state)

    ncc = nc // M_CHUNK
    assert ncc == 2
    step(ctx, start=0, direction=0, first=True)
    step(ctx, start=M_CHUNK, direction=1, first=True)
    step(ctx, start=M_CHUNK, direction=0, first=False)
    step(ctx, start=0, direction=1, first=False)

    nxc = nx // M_CHUNK

    def meet(first):
        def body(i, carry):
            step(lat, start=i * M_CHUNK, direction=0, first=first)
            step(lat, start=(nxc - 1 - i) * M_CHUNK, direction=1, first=first)
            return carry
        return body

    lax.fori_loop(0, nxc // 2, meet(True), 0, unroll=8)
    lax.fori_loop(nxc // 2, nxc, meet(False), 0, unroll=4)


def _mlstm(lat, ctx, mnorm_tab):
    bsz, nx, _ = lat[0].shape
    nc = ctx[0].shape[1]
    hd = M_HEAD_DIM

    def specs(n):
        tok = pl.BlockSpec((1, n, hd), lambda b, h: (b, 0, h))
        ch = pl.BlockSpec((1, hd, n), lambda b, h: (b, h, 0))
        return [tok, tok, ch, ch, ch, pl.BlockSpec((1, n, GATE_PAD), lambda b, h: (b, 0, 0))]

    return pl.pallas_call(
        _mlstm_kernel,
        grid=(bsz, M_HEADS),
        in_specs=specs(nx) + specs(nc) + [pl.BlockSpec((hd, 128), lambda b, h: (h, 0))],
        out_specs=[pl.BlockSpec((1, hd, nx), lambda b, h: (b, h, 0)),
                   pl.BlockSpec((1, hd, nc), lambda b, h: (b, h, 0))],
        out_shape=[jax.ShapeDtypeStruct((bsz, M_WIDTH, nx), bf16),
                   jax.ShapeDtypeStruct((bsz, M_WIDTH, nc), bf16)],
        scratch_shapes=[pltpu.VMEM((M_HEADS, 8, nx), f32), pltpu.VMEM((M_HEADS, 8, nc), f32),
                        pltpu.VMEM((hd, nx), f32), pltpu.VMEM((hd, nc), f32),
                        pltpu.VMEM((2, STATE_ROWS, hd), f32), pltpu.VMEM((2, 1, 1), f32)],
        compiler_params=_params("parallel", "arbitrary"),
        name="mlstm",
    )(*lat, *ctx, mnorm_tab)


P_HALO = 16


def _out_ab_kernel(ym_ref, xp_ref, xpp_ref, xpn_ref, zp_ref, x_ref, mod_ref, gpost_ref,
                   wo_ref, pw_ref, ps_ref, o_ref, ext_scr):
    i, nt = pl.program_id(1), pl.num_programs(1)
    tm = x_ref.shape[1]
    ext_scr[0:P_HALO, :] = jnp.where(i > 0, xpp_ref[0].astype(f32), 0.0)
    ext_scr[P_HALO:P_HALO + tm, :] = xp_ref[0].astype(f32)
    ext_scr[P_HALO + tm:, :] = jnp.where(i < nt - 1, xpn_ref[0].astype(f32), 0.0)

    n = nt * tm
    ext_rows = tm + 2 * P_HALO
    t = i * tm + lax.broadcasted_iota(jnp.int32, (tm, 1), 0)
    y = jnp.dot(ym_ref[0].T, wo_ref[0:M_WIDTH, :], preferred_element_type=f32)
    for g, win in enumerate(P_WINDOWS):
        half = win // 2
        cols = slice(g * P_GROUP_DIM, (g + 1) * P_GROUP_DIM)
        e = ext_scr[:, cols]
        run, width = e, 1
        while width < win:
            run = run + pltpu.roll(run, ext_rows - width, 0)
            width *= 2
        total = pltpu.roll(run, half, 0)[P_HALO:P_HALO + tm]
        count = (jnp.minimum(t + half, n) - jnp.maximum(t - half, 0)).astype(f32)
        pooled = total * (1.0 / count) - e[P_HALO:P_HALO + tm]
        mixed = jnp.dot(pooled.astype(bf16), pw_ref[g], preferred_element_type=f32) * ps_ref[:, cols]
        yp = (mixed * _silu(zp_ref[0, :, cols].astype(f32))).astype(bf16)
        y = y + jnp.dot(yp, wo_ref[M_WIDTH + g * P_GROUP_DIM:M_WIDTH + (g + 1) * P_GROUP_DIM, :],
                        preferred_element_type=f32)
    o_ref[0] = _gated_residual(x_ref[0], y, gpost_ref[...], mod_ref[0])


def _out_ab(ymt, xp, zp, x, mod, gpost, wo, pw, ps, tm):
    bsz, n, _ = x.shape
    nh = n // P_HALO
    th = tm // P_HALO
    return pl.pallas_call(
        _out_ab_kernel,
        grid=(bsz, n // tm),
        in_specs=[pl.BlockSpec((1, M_WIDTH, tm), lambda b, i: (b, 0, i)),
                  pl.BlockSpec((1, tm, P_WIDTH), lambda b, i: (b, i, 0)),
                  pl.BlockSpec((1, P_HALO, P_WIDTH), lambda b, i: (b, jnp.maximum(i * th - 1, 0), 0)),
                  pl.BlockSpec((1, P_HALO, P_WIDTH), lambda b, i: (b, jnp.minimum((i + 1) * th, nh - 1), 0)),
                  pl.BlockSpec((1, tm, P_WIDTH), lambda b, i: (b, i, 0)),
                  pl.BlockSpec((1, tm, D_MODEL), lambda b, i: (b, i, 0)),
                  pl.BlockSpec((1, 1, 3 * D_MODEL), lambda b, i: (b % mod.shape[0], 0, 0)),
                  pl.BlockSpec((1, D_MODEL), lambda b, i: (0, 0)),
                  pl.BlockSpec((M_WIDTH + P_WIDTH, D_MODEL), lambda b, i: (0, 0)),
                  pl.BlockSpec((len(P_WINDOWS), P_GROUP_DIM, P_GROUP_DIM), lambda b, i: (0, 0, 0)),
                  pl.BlockSpec((1, P_WIDTH), lambda b, i: (0, 0))],
        out_specs=pl.BlockSpec((1, tm, D_MODEL), lambda b, i: (b, i, 0)),
        out_shape=jax.ShapeDtypeStruct((bsz, n, D_MODEL), f32),
        scratch_shapes=[pltpu.VMEM((tm + 2 * P_HALO, P_WIDTH), f32)],
        compiler_params=_params("parallel", "parallel"),
        name="out_proj_ab",
    )(ymt, xp, xp, xp, zp, x, mod, gpost, wo, pw, ps)


def _rope(a, cos, sin):
    lane = lax.broadcasted_iota(jnp.int32, a.shape, 1)
    half = A_HEAD_DIM // 2
    partner = jnp.where(lane % A_HEAD_DIM < half,
                        pltpu.roll(a, a.shape[1] - half, 1), pltpu.roll(a, half, 1))
    return a * cos + partner * sin


LOG2E = 1.4426950408889634


def _in_c_kernel(x_ref, mod_ref, gpre_ref, wt_ref, wk_ref, cost_ref, sint_ref, cos_ref, sin_ref,
                 qt_ref, k_ref, vt_ref, zt_ref, *, rope):
    half = A_HEAD_DIM // 2
    q_scale = (A_HEAD_DIM ** -0.5) * LOG2E
    for t0 in range(0, x_ref.shape[1], SUB_ROWS):
        toks = slice(t0, t0 + SUB_ROWS)
        hb = _modulated_norm(x_ref[0, toks, :], gpre_ref[...], mod_ref[0]).astype(bf16)

        def channel_major(lo, rows):
            return lax.dot_general(wt_ref[lo:lo + rows, :], hb, NT_DIMS, preferred_element_type=f32)

        for lo in range(0, A_WIDTH, C_ROWS):
            a = channel_major(lo, C_ROWS)
            for h in range(C_ROWS // A_HEAD_DIM):
                x1 = a[h * A_HEAD_DIM:h * A_HEAD_DIM + half]
                x2 = a[h * A_HEAD_DIM + half:(h + 1) * A_HEAD_DIM]
                if rope:
                    c, s = cost_ref[:, toks], sint_ref[:, toks]
                    x1, x2 = x1 * c - x2 * s, x1 * s + x2 * c
                r0 = lo + h * A_HEAD_DIM
                qt_ref[0, r0:r0 + half, toks] = (x1 * q_scale).astype(bf16)
                qt_ref[0, r0 + half:r0 + A_HEAD_DIM, toks] = (x2 * q_scale).astype(bf16)
        vt_ref[0, :, toks] = channel_major(A_WIDTH, A_KV_WIDTH).astype(bf16)
        for lo in range(0, A_WIDTH, C_ROWS):
            zt_ref[0, lo:lo + C_ROWS, toks] = channel_major(A_WIDTH + A_KV_WIDTH + lo, C_ROWS).astype(bf16)
        for j in range(A_KV_WIDTH // 128):
            kk = jnp.dot(hb, wk_ref[:, j * 128:(j + 1) * 128], preferred_element_type=f32)
            if rope:
                kk = _rope(kk, cos_ref[toks, :], sin_ref[toks, :])
            k_ref[0, toks, j * 128:(j + 1) * 128] = kk.astype(bf16)


def _in_c(x, mod, gpre, wt, wk, cost, sint, cos, sin, tm, rope):
    bsz, n, _ = x.shape
    half = A_HEAD_DIM // 2
    return pl.pallas_call(
        functools.partial(_in_c_kernel, rope=rope),
        grid=(bsz, n // tm),
        in_specs=[pl.BlockSpec((1, tm, D_MODEL), lambda b, i: (b, i, 0)),
                  pl.BlockSpec((1, 1, 3 * D_MODEL), lambda b, i: (b % mod.shape[0], 0, 0)),
                  pl.BlockSpec((1, D_MODEL), lambda b, i: (0, 0)),
                  pl.BlockSpec(wt.shape, lambda b, i: (0, 0)),
                  pl.BlockSpec(wk.shape, lambda b, i: (0, 0)),
                  pl.BlockSpec((half, tm), lambda b, i: (0, i)),
                  pl.BlockSpec((half, tm), lambda b, i: (0, i)),
                  pl.BlockSpec((tm, 128), lambda b, i: (i, 0)),
                  pl.BlockSpec((tm, 128), lambda b, i: (i, 0))],
        out_specs=[pl.BlockSpec((1, A_WIDTH, tm), lambda b, i: (b, 0, i)),
                   pl.BlockSpec((1, tm, A_KV_WIDTH), lambda b, i: (b, i, 0)),
                   pl.BlockSpec((1, A_KV_WIDTH, tm), lambda b, i: (b, 0, i)),
                   pl.BlockSpec((1, A_WIDTH, tm), lambda b, i: (b, 0, i))],
        out_shape=[jax.ShapeDtypeStruct((bsz, A_WIDTH, n), bf16),
                   jax.ShapeDtypeStruct((bsz, n, A_KV_WIDTH), bf16),
                   jax.ShapeDtypeStruct((bsz, A_KV_WIDTH, n), bf16),
                   jax.ShapeDtypeStruct((bsz, A_WIDTH, n), bf16)],
        compiler_params=_params("parallel", "parallel"),
        name="in_proj_c",
    )(x, mod, gpre, wt, wk, cost, sint, cos, sin)


def _attn_kernel(sink_ref, qt_ref, k_ref, vt_ref, kc_ref, vct_ref, zt_ref, ogt_ref, s_scr, *, banded):
    nc = kc_ref.shape[1]
    blocks = qt_ref.shape[2] // A_BLOCK
    band_start, bias = [], []
    if banded:
        n = k_ref.shape[1]
        for qb in range(blocks):
            start = (pl.program_id(1) * blocks + qb) * A_BLOCK
            bs = pl.multiple_of(jnp.clip(start - A_WINDOW, 0, n - A_BAND), A_BLOCK)
            kj = bs + lax.broadcasted_iota(jnp.int32, (A_BAND, A_BLOCK), 0)
            qi = start + lax.broadcasted_iota(jnp.int32, (A_BAND, A_BLOCK), 1)
            mask = jnp.where(jnp.abs(qi - kj) <= A_WINDOW, 0.0, -jnp.inf)
            band_start.append(bs)
            bias.append(jnp.concatenate([mask] * A_GROUP, axis=1))

    def scores(u):
        qb, g = divmod(u, A_KV_HEADS)
        lanes = slice(qb * A_BLOCK, (qb + 1) * A_BLOCK)
        heads = range(g * A_GROUP, (g + 1) * A_GROUP)
        pair = slice(128 * (g // 2), 128 * (g // 2) + 128)
        qg = jnp.concatenate([qt_ref[0, h * A_HEAD_DIM:(h + 1) * A_HEAD_DIM, lanes] for h in heads], axis=1)
        zero = jnp.zeros_like(qg)
        qpad = jnp.concatenate([qg, zero] if g % 2 == 0 else [zero, qg], axis=0)
        slot = u % s_scr.shape[0]
        s_scr[slot, 0:nc, :] = jnp.dot(kc_ref[0, :, pair], qpad, preferred_element_type=f32)
        if banded:
            s_scr[slot, nc:, :] = jnp.dot(k_ref[0, pl.ds(band_start[qb], A_BAND), pair], qpad,
                                       preferred_element_type=f32) + bias[qb]

    def softmax(u):
        g = u % A_KV_HEADS
        heads = range(g * A_GROUP, (g + 1) * A_GROUP)
        sk = jnp.concatenate([jnp.full((1, A_BLOCK), sink_ref[h] * LOG2E, f32) for h in heads], axis=1)
        s = s_scr[u % s_scr.shape[0]]
        m = jnp.maximum(jnp.max(s, axis=0, keepdims=True), sk)
        p = jnp.exp2(s - m)
        inv = 1.0 / (jnp.sum(p, axis=0, keepdims=True) + jnp.exp2(sk - m))
        return p.astype(bf16), inv

    def output(u, pb, inv):
        qb, g = divmod(u, A_KV_HEADS)
        lanes = slice(qb * A_BLOCK, (qb + 1) * A_BLOCK)
        heads = range(g * A_GROUP, (g + 1) * A_GROUP)
        kvrows = slice(g * A_HEAD_DIM, (g + 1) * A_HEAD_DIM)
        ot = jnp.dot(vct_ref[0, kvrows, :], pb[0:nc], preferred_element_type=f32)
        if banded:
            ot = ot + jnp.dot(vt_ref[0, kvrows, pl.ds(band_start[qb], A_BAND)], pb[nc:],
                              preferred_element_type=f32)
        ot = ot * inv
        for a, h in enumerate(heads):
            rows = slice(h * A_HEAD_DIM, (h + 1) * A_HEAD_DIM)
            gate = _silu(zt_ref[0, rows, lanes].astype(f32))
            ogt_ref[0, rows, lanes] = (ot[:, a * A_BLOCK:(a + 1) * A_BLOCK] * gate).astype(bf16)

    units = blocks * A_KV_HEADS
    scores(0)
    scores(1)
    pending = None
    for u in range(units):
        sm = softmax(u)
        if u + 2 < units:
            scores(u + 2)
        if pending is not None:
            output(u - 1, *pending)
        pending = sm
    output(units - 1, *pending)


def _attn(sink, qt, k, vt, kc, vct, zt, banded):
    bsz, _, nq = qt.shape
    nk, nc = k.shape[1], kc.shape[1]
    nkeys = nc + (A_BAND if banded else 0)
    blocks = min(A_QUERY_BLOCKS, nq // A_BLOCK)
    tq = blocks * A_BLOCK
    return pl.pallas_call(
        functools.partial(_attn_kernel, banded=banded),
        grid=(bsz, nq // tq),
        in_specs=[pl.BlockSpec(memory_space=pltpu.SMEM),
                  pl.BlockSpec((1, A_WIDTH, tq), lambda b, i: (b, 0, i)),
                  pl.BlockSpec((1, nk, A_KV_WIDTH), lambda b, i: (b, 0, 0)),
                  pl.BlockSpec((1, A_KV_WIDTH, nk), lambda b, i: (b, 0, 0)),
                  pl.BlockSpec((1, nc, A_KV_WIDTH), lambda b, i: (b, 0, 0)),
                  pl.BlockSpec((1, A_KV_WIDTH, nc), lambda b, i: (b, 0, 0)),
                  pl.BlockSpec((1, A_WIDTH, tq), lambda b, i: (b, 0, i))],
        out_specs=pl.BlockSpec((1, A_WIDTH, tq), lambda b, i: (b, 0, i)),
        out_shape=jax.ShapeDtypeStruct((bsz, A_WIDTH, nq), bf16),
        scratch_shapes=[pltpu.VMEM((4, nkeys, A_GROUP * A_BLOCK), f32)],
        compiler_params=_params("parallel", "parallel"),
        name="attention_banded" if banded else "attention_context",
    )(sink, qt, k, vt, kc, vct, zt)


def _out_c_kernel(at_ref, x_ref, mod_ref, gpost_ref, wo_ref, o_ref):
    y = jnp.dot(at_ref[0].T, wo_ref[...], preferred_element_type=f32)
    o_ref[0] = _gated_residual(x_ref[0], y, gpost_ref[...], mod_ref[0])


def _out_c(at, x, mod, gpost, wo, tm):
    bsz, n, _ = x.shape
    return pl.pallas_call(
        _out_c_kernel,
        grid=(bsz, n // tm),
        in_specs=[pl.BlockSpec((1, A_WIDTH, tm), lambda b, i: (b, 0, i)),
                  pl.BlockSpec((1, tm, D_MODEL), lambda b, i: (b, i, 0)),
                  pl.BlockSpec((1, 1, 3 * D_MODEL), lambda b, i: (b % mod.shape[0], 0, 0)),
                  pl.BlockSpec((1, D_MODEL), lambda b, i: (0, 0)),
                  pl.BlockSpec((A_WIDTH, D_MODEL), lambda b, i: (0, 0))],
        out_specs=pl.BlockSpec((1, tm, D_MODEL), lambda b, i: (b, i, 0)),
        out_shape=jax.ShapeDtypeStruct((bsz, n, D_MODEL), f32),
        compiler_params=_params("parallel", "parallel"),
        name="out_proj_c",
    )(at, x, mod, gpost, wo)


def _rope_tables(n):
    rows = n // GRID_W
    row = np.repeat(np.arange(rows), GRID_W).astype(np.float64)
    col = np.tile(np.arange(GRID_W), rows).astype(np.float64)
    n_freq = A_HEAD_DIM // 4
    inv = ROPE_BASE ** (-np.arange(n_freq, dtype=np.float64) / n_freq)
    ang = np.concatenate([row[:, None] * inv, col[:, None] * inv], -1)
    cos, sin = np.cos(ang).astype(np.float32), np.sin(ang).astype(np.float32)
    cos_t = np.tile(np.concatenate([cos, cos], -1), (1, 128 // A_HEAD_DIM))
    sin_t = np.tile(np.concatenate([-sin, sin], -1), (1, 128 // A_HEAD_DIM))
    return tuple(jnp.asarray(np.ascontiguousarray(a)) for a in (cos.T, sin.T, cos_t, sin_t))


def _split_pairs_columns(w, heads):
    perm = np.concatenate([np.arange(0, A_HEAD_DIM, 2), np.arange(1, A_HEAD_DIM, 2)])
    idx = (np.arange(heads)[:, None] * A_HEAD_DIM + perm[None, :]).reshape(-1)
    return w[:, idx]


def kernel(x, c, ctx, c_ctx, w_mod, b_mod, g_pre, g_post, ab_w_in, ab_b_gate, ab_conv, ab_mnorm,
           ab_pool_w, ab_pool_scale, ab_w_out, c_w_in, c_sink, c_w_out):
    bsz, n, _ = x.shape
    n_ctx = ctx.shape[1]
    assert bsz < 16
    cs = jnp.zeros((16, D_MODEL), f32).at[:bsz].set(c).at[bsz].set(c_ctx)
    mod_all = _modulation(cs, w_mod, b_mod)
    cost, sint, cos, sin = _rope_tables(n)
    tm_x = 1024

    for l in range(DEPTH):
        j = l // 2
        last = l == DEPTH - 1
        mod_x = mod_all[l, :bsz].reshape(bsz, 1, 3 * D_MODEL)
        mod_c = mod_all[l, bsz:bsz + 1].reshape(1, 1, 3 * D_MODEL)
        gpre = g_pre[l].reshape(1, D_MODEL)
        gpost = g_post[l].reshape(1, D_MODEL)
        if l % 2 == 0:
            w = ab_w_in[j].astype(bf16)
            wt_ch = w[:, 2 * M_WIDTH:5 * M_WIDTH].T
            wg = jnp.pad(w[:, GATE_OFF:], ((0, 0), (0, GATE_PAD - N_GATE)))
            bg = jnp.pad(ab_b_gate[j], (0, GATE_PAD - N_GATE)).reshape(1, GATE_PAD)
            k_scale = jnp.concatenate([jnp.ones((M_WIDTH,), f32), jnp.full((M_WIDTH,), M_HEAD_DIM ** -0.5, f32)])
            cw = ab_conv[j] * k_scale
            mnorm_tab = jnp.broadcast_to(ab_mnorm[j][:, None], (M_WIDTH, 128))

            def project(a, mod_a, tm_tok, tm_ch):
                q, k, xp, zp, g = _in_tok(a, mod_a, gpre, w, wg, bg, cw, tm_tok)
                vt, ot, zt = _in_ch(a, mod_a, gpre, wt_ch, tm_ch)
                return (q, k, vt, ot, zt, g), xp, zp

            lat, xpx, zpx = project(x, mod_x, 1024, 1024)
            con, xpc, zpc = project(ctx, mod_c, n_ctx, n_ctx)
            ymx, ymc = _mlstm(lat, con, mnorm_tab)
            wo = ab_w_out[j].astype(bf16)
            pw = ab_pool_w[j].astype(bf16)
            ps = ab_pool_scale[j].reshape(1, P_WIDTH)
            x = _out_ab(ymx, xpx, zpx, x, mod_x, gpost, wo, pw, ps, 512)
            if not last:
                ctx = _out_ab(ymc, xpc, zpc, ctx, mod_c, gpost, wo, pw, ps, n_ctx)
        else:
            w = c_w_in[j].astype(bf16)
            wt = jnp.concatenate([_split_pairs_columns(w[:, :A_WIDTH], A_HEADS),
                                  w[:, A_WIDTH + A_KV_WIDTH:]], axis=1).T
            wk = _split_pairs_columns(w[:, A_WIDTH:A_WIDTH + A_KV_WIDTH], A_KV_HEADS)
            qx, kx, vx, zx = _in_c(x, mod_x, gpre, wt, wk, cost, sint, cos, sin, 1024, True)
            qc, kc, vc, zc = _in_c(ctx, mod_c, gpre, wt, wk, cost[:, :n_ctx], sint[:, :n_ctx],
                                   cos[:n_ctx], sin[:n_ctx], n_ctx, False)
            wo = c_w_out[j].astype(bf16)
            ax = _attn(c_sink[j], qx, kx, vx, kc, vc, zx, True)
            x = _out_c(ax, x, mod_x, gpost, wo, tm_x)
            if not last:
                ac = _attn(c_sink[j], qc, kc, vc, kc, vc, zc, False)
                ctx = _out_c(ac, ctx, mod_c, gpost, wo, n_ctx)
    return x
```

```python
import functools

import jax
import jax.numpy as jnp
import numpy as np
from jax import lax
from jax.experimental import pallas as pl
from jax.experimental.pallas import tpu as pltpu

f32 = jnp.float32
bf16 = jnp.bfloat16

D_MODEL = 1024
DEPTH = 4
EPS = 1e-6
GRID_W = 64

M_HEADS = 4
M_HEAD_DIM = 256
M_WIDTH = 1024
M_CHUNK = 128
P_WINDOWS = (2, 4, 8, 16)
P_GROUP_DIM = 256
P_WIDTH = 1024
N_GATE = 16
GATE_OFF = 5 * M_WIDTH + 2 * P_WIDTH
GATE_PAD = 128

A_HEADS = 16
A_KV_HEADS = 4
A_GROUP = 4
A_HEAD_DIM = 64
A_WIDTH = 1024
A_KV_WIDTH = 256
A_WINDOW = 128
A_BLOCK = 128
A_BAND = A_BLOCK + 2 * A_WINDOW
A_QUERY_BLOCKS = 8
ROPE_BASE = 10000.0

BF16_ROWS = 16
VMEM_LIMIT = 52 * 1024 * 1024


def _params(*sem):
    return pltpu.CompilerParams(dimension_semantics=sem, vmem_limit_bytes=VMEM_LIMIT)


def _silu(a):
    return a * jax.nn.sigmoid(a)


def _modulated_norm(x, gpre, mod):
    xn = x * lax.rsqrt(jnp.mean(x * x, -1, keepdims=True) + EPS) * gpre
    return xn * (1.0 + mod[:, D_MODEL:2 * D_MODEL]) + mod[:, :D_MODEL]


def _gated_residual(x, y, gpost, mod):
    r = y * lax.rsqrt(jnp.mean(y * y, -1, keepdims=True) + EPS) * gpost
    return x + mod[:, 2 * D_MODEL:] * r


def _mod_kernel(cs_ref, w_ref, b_ref, o_ref):
    a = _silu(cs_ref[...])
    o_ref[0] = jnp.dot(a, w_ref[0], preferred_element_type=f32,
                       precision=lax.Precision.HIGHEST) + b_ref[0]


def _modulation(cs, w_mod, b_mod):
    tn = 1024
    return pl.pallas_call(
        _mod_kernel,
        grid=(DEPTH, 3 * D_MODEL // tn),
        in_specs=[pl.BlockSpec((16, D_MODEL), lambda l, j: (0, 0)),
                  pl.BlockSpec((1, D_MODEL, tn), lambda l, j: (l, 0, j)),
                  pl.BlockSpec((1, 1, tn), lambda l, j: (l, 0, j))],
        out_specs=pl.BlockSpec((1, 16, tn), lambda l, j: (l, 0, j)),
        out_shape=jax.ShapeDtypeStruct((DEPTH, 16, 3 * D_MODEL), f32),
        compiler_params=_params("parallel", "parallel"),
        name="modulation",
    )(cs, w_mod, b_mod.reshape(DEPTH, 1, 3 * D_MODEL))


NT_DIMS = (((1,), (1,)), ((), ()))
C_ROWS = 512
HALO = 16
CONV_COLS = 256
SUB_ROWS = 256


def _in_tok_kernel(x_ref, xprev_ref, xnext_ref, mod_ref, gpre_ref, w_ref, wg_ref, bg_ref, cw_ref,
                   q_ref, k_ref, xp_ref, zp_ref, g_ref, h_scr):
    i, nt, j = pl.program_id(1), pl.num_programs(1), pl.program_id(2)
    tm = x_ref.shape[1]

    @pl.when(j == 0)
    def _():
        def normed(a):
            return _modulated_norm(a, gpre_ref[...], mod_ref[0]).astype(bf16)

        h_scr[0:HALO, :] = jnp.where(i > 0, normed(xprev_ref[0]), 0.0).astype(bf16)
        h_scr[HALO:HALO + tm, :] = normed(x_ref[0])
        h_scr[HALO + tm:, :] = jnp.where(i < nt - 1, normed(xnext_ref[0]), 0.0).astype(bf16)
        g_ref[0] = jnp.dot(h_scr[HALO:HALO + tm, :], wg_ref[...], preferred_element_type=f32) + bg_ref[...]

        ext = tm + 2 * HALO
        for half, out_ref in enumerate((q_ref, k_ref)):
            for lo in range(0, M_WIDTH, CONV_COLS):
                src = slice(half * M_WIDTH + lo, half * M_WIDTH + lo + CONV_COLS)
                r = jnp.dot(h_scr[...], w_ref[:, src], preferred_element_type=f32)
                conv = (cw_ref[0:1, src] * pltpu.roll(r, 1, 0) + cw_ref[1:2, src] * r
                        + cw_ref[2:3, src] * pltpu.roll(r, ext - 1, 0))
                out_ref[0, :, lo:lo + CONV_COLS] = conv[HALO:HALO + tm].astype(bf16)

    @pl.when(j == 1)
    def _():
        for half, out_ref in enumerate((xp_ref, zp_ref)):
            for lo in range(0, P_WIDTH, CONV_COLS):
                src = slice(half * P_WIDTH + lo, half * P_WIDTH + lo + CONV_COLS)
                out_ref[0, :, lo:lo + CONV_COLS] = jnp.dot(
                    h_scr[HALO:HALO + tm, :], w_ref[:, src], preferred_element_type=f32).astype(bf16)


def _in_tok(x, mod, gpre, w, wg, bg, cw, tm):
    bsz, n, _ = x.shape
    th, nh = tm // HALO, n // HALO
    tok = jax.ShapeDtypeStruct((bsz, n, M_WIDTH), bf16)
    tok_spec = pl.BlockSpec((1, tm, M_WIDTH), lambda b, i, j: (b, i, 0))
    return pl.pallas_call(
        _in_tok_kernel,
        grid=(bsz, n // tm, 2),
        in_specs=[pl.BlockSpec((1, tm, D_MODEL), lambda b, i, j: (b, i, 0)),
                  pl.BlockSpec((1, HALO, D_MODEL), lambda b, i, j: (b, jnp.maximum(i * th - 1, 0), 0)),
                  pl.BlockSpec((1, HALO, D_MODEL), lambda b, i, j: (b, jnp.minimum((i + 1) * th, nh - 1), 0)),
                  pl.BlockSpec((1, 1, 3 * D_MODEL), lambda b, i, j: (b % mod.shape[0], 0, 0)),
                  pl.BlockSpec((1, D_MODEL), lambda b, i, j: (0, 0)),
                  pl.BlockSpec((D_MODEL, 2 * M_WIDTH), lambda b, i, j: (0, j)),
                  pl.BlockSpec((D_MODEL, GATE_PAD), lambda b, i, j: (0, 0)),
                  pl.BlockSpec((1, GATE_PAD), lambda b, i, j: (0, 0)),
                  pl.BlockSpec((3, 2 * M_WIDTH), lambda b, i, j: (0, 0))],
        out_specs=[tok_spec, tok_spec, tok_spec, tok_spec,
                   pl.BlockSpec((1, tm, GATE_PAD), lambda b, i, j: (b, i, 0))],
        out_shape=[tok, tok, tok, tok, jax.ShapeDtypeStruct((bsz, n, GATE_PAD), f32)],
        scratch_shapes=[pltpu.VMEM((tm + 2 * HALO, D_MODEL), bf16)],
        compiler_params=_params("parallel", "parallel", "arbitrary"),
        name="in_proj_ab_tok",
    )(x, x, x, mod, gpre, w, wg, bg, cw)


def _in_ch_kernel(x_ref, mod_ref, gpre_ref, wt_ref, vt_ref, ot_ref, zt_ref):
    for t0 in range(0, x_ref.shape[1], SUB_ROWS):
        toks = slice(t0, t0 + SUB_ROWS)
        hb = _modulated_norm(x_ref[0, toks, :], gpre_ref[...], mod_ref[0]).astype(bf16)
        for idx, out_ref in enumerate((vt_ref, ot_ref, zt_ref)):
            for lo in range(0, M_WIDTH, C_ROWS):
                rows = wt_ref[idx * M_WIDTH + lo:idx * M_WIDTH + lo + C_ROWS, :]
                out_ref[0, lo:lo + C_ROWS, toks] = lax.dot_general(
                    rows, hb, NT_DIMS, preferred_element_type=f32).astype(bf16)


def _in_ch(x, mod, gpre, wt, tm):
    bsz, n, _ = x.shape
    ch = jax.ShapeDtypeStruct((bsz, M_WIDTH, n), bf16)
    ch_spec = pl.BlockSpec((1, M_WIDTH, tm), lambda b, i: (b, 0, i))
    return pl.pallas_call(
        _in_ch_kernel,
        grid=(bsz, n // tm),
        in_specs=[pl.BlockSpec((1, tm, D_MODEL), lambda b, i: (b, i, 0)),
                  pl.BlockSpec((1, 1, 3 * D_MODEL), lambda b, i: (b % mod.shape[0], 0, 0)),
                  pl.BlockSpec((1, D_MODEL), lambda b, i: (0, 0)),
                  pl.BlockSpec(wt.shape, lambda b, i: (0, 0))],
        out_specs=[ch_spec, ch_spec, ch_spec],
        out_shape=[ch, ch, ch],
        compiler_params=_params("parallel", "parallel"),
        name="in_proj_ab_ch",
    )(x, mod, gpre, wt)


def _log_sigmoid(a):
    return jnp.minimum(a, 0.0) - jnp.log1p(jnp.exp(-jnp.abs(a)))


def _gate_rows(g_ref, r_scr):
    g = g_ref[0]
    n = g.shape[0]
    sub = lax.broadcasted_iota(jnp.int32, (N_GATE, GATE_PAD), 0)
    lane = lax.broadcasted_iota(jnp.int32, (N_GATE, GATE_PAD), 1)
    pick = jnp.where(lane == (sub % 4) * M_HEADS + sub // 4, 1.0, 0.0).astype(bf16)
    rows = jnp.zeros((N_GATE, n), f32)
    rest = g
    for _ in range(3):
        piece = rest.astype(bf16)
        rest = rest - piece.astype(f32)
        rows = rows + lax.dot_general(pick, piece, NT_DIMS, preferred_element_type=f32)
    logf = _log_sigmoid(rows)
    pos = lax.broadcasted_iota(jnp.int32, (N_GATE, n), 1) % M_CHUNK
    pre, suf = logf, logf
    shift = 1
    while shift < M_CHUNK:
        pre = pre + jnp.where(pos >= shift, pltpu.roll(pre, shift, 1), 0.0)
        suf = suf + jnp.where(pos < M_CHUNK - shift, pltpu.roll(suf, n - shift, 1), 0.0)
        shift *= 2
    kind = lax.broadcasted_iota(jnp.int32, (N_GATE, n), 0) % 4
    final = jnp.where(kind == 1, pre, jnp.where(kind == 3, suf, rows))
    for h in range(M_HEADS):
        r_scr[h, 0:4, :] = final[4 * h:4 * h + 4]


STATE_ROWS = M_HEAD_DIM + BF16_ROWS


def _mlstm_chunk(seg, consts, state, start, direction, first):
    q_ref, k_ref, vt_ref, ot_ref, zt_ref, r_scr, acc, y_ref = seg
    mn_ref, head = consts
    ct_scr, m_scr = state
    if not isinstance(start, int):
        start = pl.multiple_of(start, M_CHUNK)
    span = pl.ds(start, M_CHUNK)
    dk = M_HEAD_DIM

    q = q_ref[0, span, :]
    k = k_ref[0, span, :]
    vt = vt_ref[0, :, span]

    li = r_scr[head, 2 * direction:2 * direction + 1, span]
    bcum = r_scr[head, 2 * direction + 1:2 * direction + 2, span]
    si = lax.broadcasted_iota(jnp.int32, (M_CHUNK, M_CHUNK), 0)
    ti = lax.broadcasted_iota(jnp.int32, (M_CHUNK, M_CHUNK), 1)
    causal = (si <= ti) if direction == 0 else (si >= ti)
    b_last = bcum[:, M_CHUNK - 1:M_CHUNK] if direction == 0 else bcum[:, 0:1]

    m_prev = m_scr[direction]
    ct = ct_scr[direction]
    n_mem = ct[dk:dk + 1, :]

    key_side = jnp.transpose(jnp.broadcast_to(li - bcum, (M_CHUNK, M_CHUNK)))
    dmat = jnp.where(causal, bcum + key_side, -jnp.inf)
    inter = bcum + m_prev
    m_t = jnp.maximum(inter, jnp.max(dmat, axis=0, keepdims=True))
    w_inter = jnp.exp(inter - m_t)
    s = lax.dot_general(k, q, NT_DIMS, preferred_element_type=f32) * jnp.exp(dmat - m_t)

    tail = lax.broadcasted_iota(jnp.int32, (BF16_ROWS, dk), 0)
    n_hi = n_mem.astype(bf16).astype(f32)
    extra = jnp.where(tail == 0, n_hi, jnp.where(tail == 1, n_mem - n_hi, 0.0))
    lhs = jnp.concatenate([ct[:dk].astype(bf16), extra.astype(bf16)], axis=0)
    from_state = lax.dot_general(lhs, q, NT_DIMS, preferred_element_type=f32)
    num = w_inter * from_state[:dk] + jnp.dot(vt, s.astype(bf16), preferred_element_type=f32)
    den = (w_inter * (from_state[dk:dk + 1] + from_state[dk + 1:dk + 2])
           + jnp.sum(s, axis=0, keepdims=True))
    h = num * (1.0 / jnp.maximum(jnp.abs(den), jnp.exp(-m_t)))

    d_last = b_last - bcum + li
    m_new = jnp.maximum(b_last + m_prev, jnp.max(d_last, axis=-1, keepdims=True))
    w_row = jnp.exp(d_last - m_new)
    decay = jnp.exp(b_last + m_prev - m_new)
    w_hi = w_row.astype(bf16).astype(f32)
    w_extra = jnp.where(tail[:, :M_CHUNK] == 0, w_hi, jnp.where(tail[:, :M_CHUNK] == 1, w_row - w_hi, 0.0))
    va = jnp.concatenate([(vt.astype(f32) * w_row).astype(bf16), w_extra.astype(bf16)], axis=0)
    update = jnp.dot(va, k, preferred_element_type=f32)
    ct_scr[direction, 0:dk, :] = decay * ct[:dk] + update[:dk]
    n_new = decay * n_mem + update[dk:dk + 1] + update[dk + 1:dk + 2]
    ct_scr[direction, dk:, :] = jnp.where(tail == 0, n_new, 0.0)
    m_scr[direction] = m_new

    if first:
        acc[:, span] = h
    else:
        hg = (acc[:, span] + h) * jax.nn.sigmoid(ot_ref[0, :, span].astype(f32))
        hn = hg * lax.rsqrt(jnp.mean(hg * hg, axis=0, keepdims=True) + EPS) * mn_ref[...]
        y_ref[0, :, span] = (hn * _silu(zt_ref[0, :, span].astype(f32))).astype(bf16)


def _mlstm_kernel(qx_ref, kx_ref, vx_ref, ox_ref, zx_ref, gx_ref,
                  qc_ref, kc_ref, vc_ref, oc_ref, zc_ref, gc_ref,
                  mn_ref, yx_ref, yc_ref,
                  rx_scr, rc_scr, accx, accc, ct_scr, m_scr):
    head = pl.program_id(1)
    nx, nc = qx_ref.shape[1], qc_ref.shape[1]

    @pl.when(head == 0)
    def _():
        _gate_rows(gx_ref, rx_scr)
        _gate_rows(gc_ref, rc_scr)

    ct_scr[...] = jnp.zeros_like(ct_scr)
    m_scr[...] = jnp.zeros_like(m_scr)

    consts = (mn_ref, head)
    state = (ct_scr, m_scr)
    ctx = (qc_ref, kc_ref, vc_ref, oc_ref, zc_ref, rc_scr, accc, yc_ref)
    lat = (qx_ref, kx_ref, vx_ref, ox_ref, zx_ref, rx_scr, accx, yx_ref)
    step = functools.partial(_mlstm_chunk, consts=consts, state=state)

    ncc = nc // M_CHUNK
    assert ncc == 2
    step(ctx, start=0, direction=0, first=True)
    step(ctx, start=M_CHUNK, direction=1, first=True)
    step(ctx, start=M_CHUNK, direction=0, first=False)
    step(ctx, start=0, direction=1, first=False)

    nxc = nx // M_CHUNK

    def meet(first):
        def body(i, carry):
            step(lat, start=i * M_CHUNK, direction=0, first=first)
            step(lat, start=(nxc - 1 - i) * M_CHUNK, direction=1, first=first)
            return carry
        return body

    lax.fori_loop(0, nxc // 2, meet(True), 0, unroll=8)
    lax.fori_loop(nxc // 2, nxc, meet(False), 0, unroll=4)


def _mlstm(lat, ctx, mnorm_tab):
    bsz, nx, _ = lat[0].shape
    nc = ctx[0].shape[1]
    hd = M_HEAD_DIM

    def specs(n):
        tok = pl.BlockSpec((1, n, hd), lambda b, h: (b, 0, h))
        ch = pl.BlockSpec((1, hd, n), lambda b, h: (b, h, 0))
        return [tok, tok, ch, ch, ch, pl.BlockSpec((1, n, GATE_PAD), lambda b, h: (b, 0, 0))]

    return pl.pallas_call(
        _mlstm_kernel,
        grid=(bsz, M_HEADS),
        in_specs=specs(nx) + specs(nc) + [pl.BlockSpec((hd, 128), lambda b, h: (h, 0))],
        out_specs=[pl.BlockSpec((1, hd, nx), lambda b, h: (b, h, 0)),
                   pl.BlockSpec((1, hd, nc), lambda b, h: (b, h, 0))],
        out_shape=[jax.ShapeDtypeStruct((bsz, M_WIDTH, nx), bf16),
                   jax.ShapeDtypeStruct((bsz, M_WIDTH, nc), bf16)],
        scratch_shapes=[pltpu.VMEM((M_HEADS, 8, nx), f32), pltpu.VMEM((M_HEADS, 8, nc), f32),
                        pltpu.VMEM((hd, nx), f32), pltpu.VMEM((hd, nc), f32),
                        pltpu.VMEM((2, STATE_ROWS, hd), f32), pltpu.VMEM((2, 1, 1), f32)],
        compiler_params=_params("parallel", "arbitrary"),
        name="mlstm",
    )(*lat, *ctx, mnorm_tab)


P_HALO = 16


def _out_ab_kernel(ym_ref, xp_ref, xpp_ref, xpn_ref, zp_ref, x_ref, mod_ref, gpost_ref,
                   wo_ref, pw_ref, ps_ref, o_ref, ext_scr):
    i, nt = pl.program_id(1), pl.num_programs(1)
    tm = x_ref.shape[1]
    ext_scr[0:P_HALO, :] = jnp.where(i > 0, xpp_ref[0].astype(f32), 0.0)
    ext_scr[P_HALO:P_HALO + tm, :] = xp_ref[0].astype(f32)
    ext_scr[P_HALO + tm:, :] = jnp.where(i < nt - 1, xpn_ref[0].astype(f32), 0.0)

    n = nt * tm
    ext_rows = tm + 2 * P_HALO
    t = i * tm + lax.broadcasted_iota(jnp.int32, (tm, 1), 0)
    y = jnp.dot(ym_ref[0].T, wo_ref[0:M_WIDTH, :], preferred_element_type=f32)
    for g, win in enumerate(P_WINDOWS):
        half = win // 2
        cols = slice(g * P_GROUP_DIM, (g + 1) * P_GROUP_DIM)
        e = ext_scr[:, cols]
        run, width = e, 1
        while width < win:
            run = run + pltpu.roll(run, ext_rows - width, 0)
            width *= 2
        total = pltpu.roll(run, half, 0)[P_HALO:P_HALO + tm]
        count = (jnp.minimum(t + half, n) - jnp.maximum(t - half, 0)).astype(f32)
        pooled = total * (1.0 / count) - e[P_HALO:P_HALO + tm]
        mixed = jnp.dot(pooled.astype(bf16), pw_ref[g], preferred_element_type=f32) * ps_ref[:, cols]
        yp = (mixed * _silu(zp_ref[0, :, cols].astype(f32))).astype(bf16)
        y = y + jnp.dot(yp, wo_ref[M_WIDTH + g * P_GROUP_DIM:M_WIDTH + (g + 1) * P_GROUP_DIM, :],
                        preferred_element_type=f32)
    o_ref[0] = _gated_residual(x_ref[0], y, gpost_ref[...], mod_ref[0])


def _out_ab(ymt, xp, zp, x, mod, gpost, wo, pw, ps, tm):
    bsz, n, _ = x.shape
    nh = n // P_HALO
    th = tm // P_HALO
    return pl.pallas_call(
        _out_ab_kernel,
        grid=(bsz, n // tm),
        in_specs=[pl.BlockSpec((1, M_WIDTH, tm), lambda b, i: (b, 0, i)),
                  pl.BlockSpec((1, tm, P_WIDTH), lambda b, i: (b, i, 0)),
                  pl.BlockSpec((1, P_HALO, P_WIDTH), lambda b, i: (b, jnp.maximum(i * th - 1, 0), 0)),
                  pl.BlockSpec((1, P_HALO, P_WIDTH), lambda b, i: (b, jnp.minimum((i + 1) * th, nh - 1), 0)),
                  pl.BlockSpec((1, tm, P_WIDTH), lambda b, i: (b, i, 0)),
                  pl.BlockSpec((1, tm, D_MODEL), lambda b, i: (b, i, 0)),
                  pl.BlockSpec((1, 1, 3 * D_MODEL), lambda b, i: (b % mod.shape[0], 0, 0)),
                  pl.BlockSpec((1, D_MODEL), lambda b, i: (0, 0)),
                  pl.BlockSpec((M_WIDTH + P_WIDTH, D_MODEL), lambda b, i: (0, 0)),
                  pl.BlockSpec((len(P_WINDOWS), P_GROUP_DIM, P_GROUP_DIM), lambda b, i: (0, 0, 0)),
                  pl.BlockSpec((1, P_WIDTH), lambda b, i: (0, 0))],
        out_specs=pl.BlockSpec((1, tm, D_MODEL), lambda b, i: (b, i, 0)),
        out_shape=jax.ShapeDtypeStruct((bsz, n, D_MODEL), f32),
        scratch_shapes=[pltpu.VMEM((tm + 2 * P_HALO, P_WIDTH), f32)],
        compiler_params=_params("parallel", "parallel"),
        name="out_proj_ab",
    )(ymt, xp, xp, xp, zp, x, mod, gpost, wo, pw, ps)


def _rope(a, cos, sin):
    lane = lax.broadcasted_iota(jnp.int32, a.shape, 1)
    half = A_HEAD_DIM // 2
    partner = jnp.where(lane % A_HEAD_DIM < half,
                        pltpu.roll(a, a.shape[1] - half, 1), pltpu.roll(a, half, 1))
    return a * cos + partner * sin


LOG2E = 1.4426950408889634


def _in_c_kernel(x_ref, mod_ref, gpre_ref, wt_ref, wk_ref, cost_ref, sint_ref, cos_ref, sin_ref,
                 qt_ref, k_ref, vt_ref, zt_ref, *, rope):
    half = A_HEAD_DIM // 2
    q_scale = (A_HEAD_DIM ** -0.5) * LOG2E
    for t0 in range(0, x_ref.shape[1], SUB_ROWS):
        toks = slice(t0, t0 + SUB_ROWS)
        hb = _modulated_norm(x_ref[0, toks, :], gpre_ref[...], mod_ref[0]).astype(bf16)

        def channel_major(lo, rows):
            return lax.dot_general(wt_ref[lo:lo + rows, :], hb, NT_DIMS, preferred_element_type=f32)

        for lo in range(0, A_WIDTH, C_ROWS):
            a = channel_major(lo, C_ROWS)
            for h in range(C_ROWS // A_HEAD_DIM):
                x1 = a[h * A_HEAD_DIM:h * A_HEAD_DIM + half]
                x2 = a[h * A_HEAD_DIM + half:(h + 1) * A_HEAD_DIM]
                if rope:
                    c, s = cost_ref[:, toks], sint_ref[:, toks]
                    x1, x2 = x1 * c - x2 * s, x1 * s + x2 * c
                r0 = lo + h * A_HEAD_DIM
                qt_ref[0, r0:r0 + half, toks] = (x1 * q_scale).astype(bf16)
                qt_ref[0, r0 + half:r0 + A_HEAD_DIM, toks] = (x2 * q_scale).astype(bf16)
        vt_ref[0, :, toks] = channel_major(A_WIDTH, A_KV_WIDTH).astype(bf16)
        for lo in range(0, A_WIDTH, C_ROWS):
            zt_ref[0, lo:lo + C_ROWS, toks] = channel_major(A_WIDTH + A_KV_WIDTH + lo, C_ROWS).astype(bf16)
        for j in range(A_KV_WIDTH // 128):
            kk = jnp.dot(hb, wk_ref[:, j * 128:(j + 1) * 128], preferred_element_type=f32)
            if rope:
                kk = _rope(kk, cos_ref[toks, :], sin_ref[toks, :])
            k_ref[0, toks, j * 128:(j + 1) * 128] = kk.astype(bf16)


def _in_c(x, mod, gpre, wt, wk, cost, sint, cos, sin, tm, rope):
    bsz, n, _ = x.shape
    half = A_HEAD_DIM // 2
    return pl.pallas_call(
        functools.partial(_in_c_kernel, rope=rope),
        grid=(bsz, n // tm),
        in_specs=[pl.BlockSpec((1, tm, D_MODEL), lambda b, i: (b, i, 0)),
                  pl.BlockSpec((1, 1, 3 * D_MODEL), lambda b, i: (b % mod.shape[0], 0, 0)),
                  pl.BlockSpec((1, D_MODEL), lambda b, i: (0, 0)),
                  pl.BlockSpec(wt.shape, lambda b, i: (0, 0)),
                  pl.BlockSpec(wk.shape, lambda b, i: (0, 0)),
                  pl.BlockSpec((half, tm), lambda b, i: (0, i)),
                  pl.BlockSpec((half, tm), lambda b, i: (0, i)),
                  pl.BlockSpec((tm, 128), lambda b, i: (i, 0)),
                  pl.BlockSpec((tm, 128), lambda b, i: (i, 0))],
        out_specs=[pl.BlockSpec((1, A_WIDTH, tm), lambda b, i: (b, 0, i)),
                   pl.BlockSpec((1, tm, A_KV_WIDTH), lambda b, i: (b, i, 0)),
                   pl.BlockSpec((1, A_KV_WIDTH, tm), lambda b, i: (b, 0, i)),
                   pl.BlockSpec((1, A_WIDTH, tm), lambda b, i: (b, 0, i))],
        out_shape=[jax.ShapeDtypeStruct((bsz, A_WIDTH, n), bf16),
                   jax.ShapeDtypeStruct((bsz, n, A_KV_WIDTH), bf16),
                   jax.ShapeDtypeStruct((bsz, A_KV_WIDTH, n), bf16),
                   jax.ShapeDtypeStruct((bsz, A_WIDTH, n), bf16)],
        compiler_params=_params("parallel", "parallel"),
        name="in_proj_c",
    )(x, mod, gpre, wt, wk, cost, sint, cos, sin)


def _attn_kernel(sink_ref, qt_ref, k_ref, vt_ref, kc_ref, vct_ref, zt_ref, ogt_ref, s_scr, *, banded):
    nc = kc_ref.shape[1]
    blocks = qt_ref.shape[2] // A_BLOCK
    band_start, bias = [], []
    if banded:
        n = k_ref.shape[1]
        for qb in range(blocks):
            start = (pl.program_id(1) * blocks + qb) * A_BLOCK
            bs = pl.multiple_of(jnp.clip(start - A_WINDOW, 0, n - A_BAND), A_BLOCK)
            kj = bs + lax.broadcasted_iota(jnp.int32, (A_BAND, A_BLOCK), 0)
            qi = start + lax.broadcasted_iota(jnp.int32, (A_BAND, A_BLOCK), 1)
            mask = jnp.where(jnp.abs(qi - kj) <= A_WINDOW, 0.0, -jnp.inf)
            band_start.append(bs)
            bias.append(jnp.concatenate([mask] * A_GROUP, axis=1))

    def scores(u):
        qb, g = divmod(u, A_KV_HEADS)
        lanes = slice(qb * A_BLOCK, (qb + 1) * A_BLOCK)
        heads = range(g * A_GROUP, (g + 1) * A_GROUP)
        pair = slice(128 * (g // 2), 128 * (g // 2) + 128)
        qg = jnp.concatenate([qt_ref[0, h * A_HEAD_DIM:(h + 1) * A_HEAD_DIM, lanes] for h in heads], axis=1)
        zero = jnp.zeros_like(qg)
        qpad = jnp.concatenate([qg, zero] if g % 2 == 0 else [zero, qg], axis=0)
        slot = u % s_scr.shape[0]
        s_scr[slot, 0:nc, :] = jnp.dot(kc_ref[0, :, pair], qpad, preferred_element_type=f32)
        if banded:
            s_scr[slot, nc:, :] = jnp.dot(k_ref[0, pl.ds(band_start[qb], A_BAND), pair], qpad,
                                       preferred_element_type=f32) + bias[qb]

    def softmax(u):
        g = u % A_KV_HEADS
        heads = range(g * A_GROUP, (g + 1) * A_GROUP)
        sk = jnp.concatenate([jnp.full((1, A_BLOCK), sink_ref[h] * LOG2E, f32) for h in heads], axis=1)
        s = s_scr[u % s_scr.shape[0]]
        m = jnp.maximum(jnp.max(s, axis=0, keepdims=True), sk)
        p = jnp.exp2(s - m)
        inv = 1.0 / (jnp.sum(p, axis=0, keepdims=True) + jnp.exp2(sk - m))
        return p.astype(bf16), inv

    def output(u, pb, inv):
        qb, g = divmod(u, A_KV_HEADS)
        lanes = slice(qb * A_BLOCK, (qb + 1) * A_BLOCK)
        heads = range(g * A_GROUP, (g + 1) * A_GROUP)
        kvrows = slice(g * A_HEAD_DIM, (g + 1) * A_HEAD_DIM)
        ot = jnp.dot(vct_ref[0, kvrows, :], pb[0:nc], preferred_element_type=f32)
        if banded:
            ot = ot + jnp.dot(vt_ref[0, kvrows, pl.ds(band_start[qb], A_BAND)], pb[nc:],
                              preferred_element_type=f32)
        ot = ot * inv
        for a, h in enumerate(heads):
            rows = slice(h * A_HEAD_DIM, (h + 1) * A_HEAD_DIM)
            gate = _silu(zt_ref[0, rows, lanes].astype(f32))
            ogt_ref[0, rows, lanes] = (ot[:, a * A_BLOCK:(a + 1) * A_BLOCK] * gate).astype(bf16)

    units = blocks * A_KV_HEADS
    scores(0)
    scores(1)
    pending = None
    for u in range(units):
        sm = softmax(u)
        if u + 2 < units:
            scores(u + 2)
        if pending is not None:
            output(u - 1, *pending)
        pending = sm
    output(units - 1, *pending)


def _attn(sink, qt, k, vt, kc, vct, zt, banded):
    bsz, _, nq = qt.shape
    nk, nc = k.shape[1], kc.shape[1]
    nkeys = nc + (A_BAND if banded else 0)
    blocks = min(A_QUERY_BLOCKS, nq // A_BLOCK)
    tq = blocks * A_BLOCK
    return pl.pallas_call(
        functools.partial(_attn_kernel, banded=banded),
        grid=(bsz, nq // tq),
        in_specs=[pl.BlockSpec(memory_space=pltpu.SMEM),
                  pl.BlockSpec((1, A_WIDTH, tq), lambda b, i: (b, 0, i)),
                  pl.BlockSpec((1, nk, A_KV_WIDTH), lambda b, i: (b, 0, 0)),
                  pl.BlockSpec((1, A_KV_WIDTH, nk), lambda b, i: (b, 0, 0)),
                  pl.BlockSpec((1, nc, A_KV_WIDTH), lambda b, i: (b, 0, 0)),
                  pl.BlockSpec((1, A_KV_WIDTH, nc), lambda b, i: (b, 0, 0)),
                  pl.BlockSpec((1, A_WIDTH, tq), lambda b, i: (b, 0, i))],
        out_specs=pl.BlockSpec((1, A_WIDTH, tq), lambda b, i: (b, 0, i)),
        out_shape=jax.ShapeDtypeStruct((bsz, A_WIDTH, nq), bf16),
        scratch_shapes=[pltpu.VMEM((4, nkeys, A_GROUP * A_BLOCK), f32)],
        compiler_params=_params("parallel", "parallel"),
        name="attention_banded" if banded else "attention_context",
    )(sink, qt, k, vt, kc, vct, zt)


def _out_c_kernel(at_ref, x_ref, mod_ref, gpost_ref, wo_ref, o_ref):
    y = jnp.dot(at_ref[0].T, wo_ref[...], preferred_element_type=f32)
    o_ref[0] = _gated_residual(x_ref[0], y, gpost_ref[...], mod_ref[0])


def _out_c(at, x, mod, gpost, wo, tm):
    bsz, n, _ = x.shape
    return pl.pallas_call(
        _out_c_kernel,
        grid=(bsz, n // tm),
        in_specs=[pl.BlockSpec((1, A_WIDTH, tm), lambda b, i: (b, 0, i)),
                  pl.BlockSpec((1, tm, D_MODEL), lambda b, i: (b, i, 0)),
                  pl.BlockSpec((1, 1, 3 * D_MODEL), lambda b, i: (b % mod.shape[0], 0, 0)),
                  pl.BlockSpec((1, D_MODEL), lambda b, i: (0, 0)),
                  pl.BlockSpec((A_WIDTH, D_MODEL), lambda b, i: (0, 0))],
        out_specs=pl.BlockSpec((1, tm, D_MODEL), lambda b, i: (b, i, 0)),
        out_shape=jax.ShapeDtypeStruct((bsz, n, D_MODEL), f32),
        compiler_params=_params("parallel", "parallel"),
        name="out_proj_c",
    )(at, x, mod, gpost, wo)


def _rope_tables(n):
    rows = n // GRID_W
    row = np.repeat(np.arange(rows), GRID_W).astype(np.float64)
    col = np.tile(np.arange(GRID_W), rows).astype(np.float64)
    n_freq = A_HEAD_DIM // 4
    inv = ROPE_BASE ** (-np.arange(n_freq, dtype=np.float64) / n_freq)
    ang = np.concatenate([row[:, None] * inv, col[:, None] * inv], -1)
    cos, sin = np.cos(ang).astype(np.float32), np.sin(ang).astype(np.float32)
    cos_t = np.tile(np.concatenate([cos, cos], -1), (1, 128 // A_HEAD_DIM))
    sin_t = np.tile(np.concatenate([-sin, sin], -1), (1, 128 // A_HEAD_DIM))
    return tuple(jnp.asarray(np.ascontiguousarray(a)) for a in (cos.T, sin.T, cos_t, sin_t))


def _split_pairs_columns(w, heads):
    perm = np.concatenate([np.arange(0, A_HEAD_DIM, 2), np.arange(1, A_HEAD_DIM, 2)])
    idx = (np.arange(heads)[:, None] * A_HEAD_DIM + perm[None, :]).reshape(-1)
    return w[:, idx]


def kernel(x, c, ctx, c_ctx, w_mod, b_mod, g_pre, g_post, ab_w_in, ab_b_gate, ab_conv, ab_mnorm,
           ab_pool_w, ab_pool_scale, ab_w_out, c_w_in, c_sink, c_w_out):
    bsz, n, _ = x.shape
    n_ctx = ctx.shape[1]
    assert bsz < 16
    cs = jnp.zeros((16, D_MODEL), f32).at[:bsz].set(c).at[bsz].set(c_ctx)
    mod_all = _modulation(cs, w_mod, b_mod)
    cost, sint, cos, sin = _rope_tables(n)
    tm_x = 1024

    for l in range(DEPTH):
        j = l // 2
        last = l == DEPTH - 1
        mod_x = mod_all[l, :bsz].reshape(bsz, 1, 3 * D_MODEL)
        mod_c = mod_all[l, bsz:bsz + 1].reshape(1, 1, 3 * D_MODEL)
        gpre = g_pre[l].reshape(1, D_MODEL)
        gpost = g_post[l].reshape(1, D_MODEL)
        if l % 2 == 0:
            w = ab_w_in[j].astype(bf16)
            w_tok = jnp.concatenate([w[:, :2 * M_WIDTH], w[:, 5 * M_WIDTH:GATE_OFF]], axis=1)
            wt_ch = w[:, 2 * M_WIDTH:5 * M_WIDTH].T
            wg = jnp.pad(w[:, GATE_OFF:], ((0, 0), (0, GATE_PAD - N_GATE)))
            bg = jnp.pad(ab_b_gate[j], (0, GATE_PAD - N_GATE)).reshape(1, GATE_PAD)
            k_scale = jnp.concatenate([jnp.ones((M_WIDTH,), f32), jnp.full((M_WIDTH,), M_HEAD_DIM ** -0.5, f32)])
            cw = ab_conv[j] * k_scale
            mnorm_tab = jnp.broadcast_to(ab_mnorm[j][:, None], (M_WIDTH, 128))

            def project(a, mod_a, tm_tok, tm_ch):
                q, k, xp, zp, g = _in_tok(a, mod_a, gpre, w_tok, wg, bg, cw, tm_tok)
                vt, ot, zt = _in_ch(a, mod_a, gpre, wt_ch, tm_ch)
                return (q, k, vt, ot, zt, g), xp, zp

            lat, xpx, zpx = project(x, mod_x, 1024, 1024)
            con, xpc, zpc = project(ctx, mod_c, n_ctx, n_ctx)
            ymx, ymc = _mlstm(lat, con, mnorm_tab)
            wo = ab_w_out[j].astype(bf16)
            pw = ab_pool_w[j].astype(bf16)
            ps = ab_pool_scale[j].reshape(1, P_WIDTH)
            x = _out_ab(ymx, xpx, zpx, x, mod_x, gpost, wo, pw, ps, 512)
            if not last:
                ctx = _out_ab(ymc, xpc, zpc, ctx, mod_c, gpost, wo, pw, ps, n_ctx)
        else:
            w = c_w_in[j].astype(bf16)
            wt = jnp.concatenate([_split_pairs_columns(w[:, :A_WIDTH], A_HEADS),
                                  w[:, A_WIDTH + A_KV_WIDTH:]], axis=1).T
            wk = _split_pairs_columns(w[:, A_WIDTH:A_WIDTH + A_KV_WIDTH], A_KV_HEADS)
            qx, kx, vx, zx = _in_c(x, mod_x, gpre, wt, wk, cost, sint, cos, sin, 1024, True)
            qc, kc, vc, zc = _in_c(ctx, mod_c, gpre, wt, wk, cost[:, :n_ctx], sint[:, :n_ctx],
                                   cos[:n_ctx], sin[:n_ctx], n_ctx, False)
            wo = c_w_out[j].astype(bf16)
            ax = _attn(c_sink[j], qx, kx, vx, kc, vc, zx, True)
            x = _out_c(ax, x, mod_x, gpost, wo, tm_x)
            if not last:
                ac = _attn(c_sink[j], qc, kc, vc, kc, vc, zc, False)
                ctx = _out_c(ac, ctx, mod_c, gpost, wo, n_ctx)
    return x
```
